```python
import jax
import jax.numpy as jnp
from jax import lax
import numpy as np

D_MODEL = 1024
BATCH = 4
SEQ = 4096
DEPTH = 4

GRID_W = 64
CTX_LEN = 256
HEAD_DIM = 64
EPS = 1e-6

A_HEADS = 8
A_KV_HEADS = 2
A_GROUP = A_HEADS // A_KV_HEADS
A_BLOCK = 128
ROPE_THETA = 10000.0

LRU_WIDTH = 512
LRU_BLOCKS = 8
LRU_BLOCK_W = LRU_WIDTH // LRU_BLOCKS
LRU_C = 8.0

CONV_W = 4
CONV_PAD = ((CONV_W - 1) // 2, CONV_W // 2)

DN_HEADS = 4
DN_DK = 128
DN_DV = 128
DN_CHUNK = 64

NA_HEADS = 8
NA_WIN_R = 8
NA_WIN_C = 16

N_BRANCH = 4
BRANCH_W = 512

N_EXPERTS = 32
TOP_K = 4
D_EXPERT = D_MODEL
SWIGLU_LIMIT = 7.0
SWIGLU_ALPHA = 1.702

IN_SPLITS = (
    ('a_q', A_HEADS * HEAD_DIM), ('a_k', A_KV_HEADS * HEAD_DIM), ('a_v', A_KV_HEADS * HEAD_DIM),
    ('b_x', LRU_WIDTH), ('b_y', LRU_WIDTH),
    ('c_q', DN_HEADS * DN_DK), ('c_k', DN_HEADS * DN_DK), ('c_v', DN_HEADS * DN_DV),
    ('c_z', DN_HEADS * DN_DV), ('c_a', 2 * DN_HEADS), ('c_b', 2 * DN_HEADS),
    ('d_q', NA_HEADS * HEAD_DIM), ('d_k', NA_HEADS * HEAD_DIM), ('d_v', NA_HEADS * HEAD_DIM),
    ('gate', N_BRANCH * D_MODEL),
)
IN_WIDTH = sum(w for _, w in IN_SPLITS)

kernel_name = 'hybrid_flow_backbone_block'


def rms_norm(x, g):
    xf = x.astype(jnp.float32)
    y = xf * lax.rsqrt(jnp.mean(xf * xf, axis=-1, keepdims=True) + EPS)
    return (y * g.astype(jnp.float32)).astype(x.dtype)


def l2_normalize(x):
    xf = x.astype(jnp.float32)
    return xf * lax.rsqrt(jnp.sum(xf * xf, axis=-1, keepdims=True) + EPS)


def modulate(h, shift, scale):
    return h * (1 + scale) + shift


def split_in(u):
    offs = np.cumsum([w for _, w in IN_SPLITS])[:-1].tolist()
    return dict(zip([n for n, _ in IN_SPLITS], jnp.split(u, offs, axis=-1)))


def flip_seq(ts):
    return tuple(jnp.flip(t, axis=1) for t in ts)


def dwconv(x, w, b=None):
    y = lax.conv_general_dilated(x, w[:, None, :].astype(x.dtype), window_strides=(1,),
                                 padding=[CONV_PAD], dimension_numbers=('NWC', 'WIO', 'NWC'),
                                 feature_group_count=x.shape[-1])
    return y if b is None else y + b.astype(x.dtype)


def axial_rope_tables(n_tokens):
    t = jnp.arange(n_tokens)
    row = (t // GRID_W).astype(jnp.float32)
    col = (t % GRID_W).astype(jnp.float32)
    n_freq = HEAD_DIM // 4
    inv = ROPE_THETA ** (-jnp.arange(n_freq, dtype=jnp.float32) / n_freq)
    ang = jnp.concatenate([row[:, None] * inv[None], col[:, None] * inv[None]], axis=-1)
    return jnp.cos(ang), jnp.sin(ang)


def apply_rope(x, cos, sin):
    half = x.shape[-1] // 2
    shp = (x.shape[1],) + (1,) * (x.ndim - 3) + (half,)
    cos, sin = cos.reshape(shp), sin.reshape(shp)
    xf = x.astype(jnp.float32)
    x1, x2 = xf[..., :half], xf[..., half:]
    return jnp.concatenate([x1 * cos - x2 * sin, x2 * cos + x1 * sin], axis=-1).astype(x.dtype)


def gqa_dense(q, k, v):
    s = jnp.einsum('bqkgd,blkd->bkgql', q, k) * (q.shape[-1] ** -0.5)
    p = jax.nn.softmax(s.astype(jnp.float32), axis=-1).astype(v.dtype)
    return jnp.einsum('bkgql,blkd->bqkgd', p, v)


def mixer_gqa(q, k, v, qc, kc, vc, g_q, g_k, cos, sin, ctx_out):
    B, S, _ = q.shape
    L = qc.shape[1]
    q = q.reshape(B, S, A_KV_HEADS, A_GROUP, HEAD_DIM)
    k = k.reshape(B, S, A_KV_HEADS, HEAD_DIM)
    v = v.reshape(B, S, A_KV_HEADS, HEAD_DIM)
    qc = rms_norm(qc.reshape(B, L, A_KV_HEADS, A_GROUP, HEAD_DIM), g_q)
    kc = rms_norm(kc.reshape(B, L, A_KV_HEADS, HEAD_DIM), g_k)
    vc = vc.reshape(B, L, A_KV_HEADS, HEAD_DIM)
    q = apply_rope(rms_norm(q, g_q), cos, sin)
    k = apply_rope(rms_norm(k, g_k), cos, sin)
    scale = HEAD_DIM ** -0.5

    def block(qb):
        s = jnp.concatenate([jnp.einsum('bqkgd,bskd->bkgqs', qb, k),
                             jnp.einsum('bqkgd,blkd->bkgql', qb, kc)], axis=-1) * scale
        p = jax.nn.softmax(s.astype(jnp.float32), axis=-1).astype(v.dtype)
        return (jnp.einsum('bkgqs,bskd->bqkgd', p[..., :S], v)
                + jnp.einsum('bkgql,blkd->bqkgd', p[..., S:], vc))

    qb = q.reshape(B, S // A_BLOCK, A_BLOCK, A_KV_HEADS, A_GROUP, HEAD_DIM).swapaxes(0, 1)
    o = lax.map(block, qb).swapaxes(0, 1).reshape(B, S, A_HEADS * HEAD_DIM)
    oc = gqa_dense(qc, kc, vc).reshape(B, L, A_HEADS * HEAD_DIM) if ctx_out else None
    return o, oc


def _rglru_coeffs(x, lam, w_r, b_r, w_i, b_i):
    xf = x.astype(jnp.float32)
    blk = xf.reshape(x.shape[:-1] + (LRU_BLOCKS, LRU_BLOCK_W))
    r = jax.nn.sigmoid(jnp.einsum('blnc,ncd->blnd', blk, w_r).reshape(x.shape) + b_r)
    i = jax.nn.sigmoid(jnp.einsum('blnc,ncd->blnd', blk, w_i).reshape(x.shape) + b_i)
    log_a = -LRU_C * r * jax.nn.softplus(-lam.astype(jnp.float32))
    return jnp.exp(log_a), jnp.sqrt(-jnp.expm1(2 * log_a)) * (i * xf)


def _lin_combine(e1, e2):
    a1, b1 = e1
    a2, b2 = e2
    return a1 * a2, a2 * b1 + b2


def _linear_scan(a, b, h0):
    b = b.at[:, 0].add(a[:, 0] * h0)
    _, h = lax.associative_scan(_lin_combine, (a, b), axis=1)
    return h


def mixer_rglru(xb, yb, xbc, ybc, conv_w, conv_b, lam, w_r, b_r, w_i, b_i, ctx_out):
    xb = dwconv(xb, conv_w, conv_b)
    xbc = dwconv(xbc, conv_w, conv_b)
    h0 = jnp.zeros((xb.shape[0], LRU_WIDTH), jnp.float32)
    hs, hcs = [], []
    for d in range(2):
        a, bb = _rglru_coeffs(xb, lam[d], w_r[d], b_r[d], w_i[d], b_i[d])
        ac, bc = _rglru_coeffs(xbc, lam[d], w_r[d], b_r[d], w_i[d], b_i[d])
        if d == 1:
            a, bb, ac, bc = flip_seq((a, bb, ac, bc))
        hc = _linear_scan(ac, bc, h0)
        h = _linear_scan(a, bb, hc[:, -1])
        if d == 1:
            h, hc = flip_seq((h, hc))
        hs.append(h)
        hcs.append(hc)
    y = ((hs[0] + hs[1]) * jax.nn.gelu(yb.astype(jnp.float32))).astype(yb.dtype)
    yc = ((hcs[0] + hcs[1]) * jax.nn.gelu(ybc.astype(jnp.float32))).astype(ybc.dtype) if ctx_out else None
    return y, yc


def chunk_gated_delta(q, k, v, g, beta, s0):
    B, L, H, _ = q.shape
    DV = v.shape[-1]
    C = DN_CHUNK
    n = L // C

    def blk(t):
        return jnp.moveaxis(t.reshape((B, n, C, H) + t.shape[3:]), 3, 1)

    q, k, v, beta = blk(q), blk(k), blk(v), blk(beta)
    gc = jnp.cumsum(blk(g), axis=-1)
    kb = k * beta[..., None]
    vb = v * beta[..., None]
    lower = np.tril(np.ones((C, C), dtype=bool))
    diff = gc[..., :, None] - gc[..., None, :]
    decay = jnp.where(lower, jnp.exp(jnp.where(lower, diff, 0.0)), 0.0)
    lmat = jnp.einsum('bhncd,bhned->bhnce', kb, k) * decay
    u = lax.linalg.triangular_solve(lmat, vb, left_side=True, lower=True, unit_diagonal=True)
    w = lax.linalg.triangular_solve(lmat, kb * jnp.exp(gc)[..., None], left_side=True, lower=True,
                                    unit_diagonal=True)
    qk = jnp.einsum('bhncd,bhned->bhnce', q, k) * decay
    qg = q * jnp.exp(gc)[..., None]
    kd = k * jnp.exp(gc[..., -1:] - gc)[..., None]
    glast = jnp.exp(gc[..., -1])

    def step(s, xs):
        qk_i, qg_i, w_i, u_i, kd_i, gl_i = xs
        vn = u_i - jnp.einsum('bhcd,bhde->bhce', w_i, s)
        o = jnp.einsum('bhcd,bhde->bhce', qg_i, s) + jnp.einsum('bhce,bhef->bhcf', qk_i, vn)
        s = s * gl_i[..., None, None] + jnp.einsum('bhcd,bhce->bhde', kd_i, vn)
        return s, o

    xs = tuple(jnp.moveaxis(t, 2, 0) for t in (qk, qg, w, u, kd, glast))
    s_fin, o = lax.scan(step, s0, xs)
    o = jnp.moveaxis(jnp.moveaxis(o, 0, 2), 1, 3).reshape(B, L, H, DV)
    return o, s_fin


def _gdn_prep(q, k, v, ga, gb, conv_w, a_log, dt_bias):
    B, L, _ = q.shape
    nk = DN_HEADS * DN_DK
    qkv = jax.nn.silu(dwconv(jnp.concatenate([q, k, v], axis=-1), conv_w).astype(jnp.float32))
    qh = l2_normalize(qkv[..., :nk].reshape(B, L, DN_HEADS, DN_DK)) * (DN_DK ** -0.5)
    kh = l2_normalize(qkv[..., nk:2 * nk].reshape(B, L, DN_HEADS, DN_DK))
    vh = qkv[..., 2 * nk:].reshape(B, L, DN_HEADS, DN_DV)
    g = -jnp.exp(a_log.astype(jnp.float32)) * jax.nn.softplus(
        ga.astype(jnp.float32).reshape(B, L, 2, DN_HEADS) + dt_bias.astype(jnp.float32))
    beta = jax.nn.sigmoid(gb.astype(jnp.float32).reshape(B, L, 2, DN_HEADS))
    return qh, kh, vh, g, beta


def _gdn_out(o, z, g_out):
    B, L, _ = z.shape
    zf = jax.nn.silu(z.astype(jnp.float32).reshape(B, L, DN_HEADS, DN_DV))
    return (rms_norm(o, g_out) * zf).reshape(B, L, DN_HEADS * DN_DV).astype(z.dtype)


def mixer_gdn(q, k, v, z, ga, gb, qc, kc, vc, zc, gac, gbc, conv_w, a_log, dt_bias, g_out, ctx_out):
    qh, kh, vh, g, beta = _gdn_prep(q, k, v, ga, gb, conv_w, a_log, dt_bias)
    qch, kch, vch, gcx, bcx = _gdn_prep(qc, kc, vc, gac, gbc, conv_w, a_log, dt_bias)
    s_zero = jnp.zeros((q.shape[0], DN_HEADS, DN_DK, DN_DV), jnp.float32)
    outs, outs_c = [], []
    for d in range(2):
        lat_in = (qh, kh, vh, g[:, :, d], beta[:, :, d])
        ctx_in = (qch, kch, vch, gcx[:, :, d], bcx[:, :, d])
        if d == 1:
            lat_in, ctx_in = flip_seq(lat_in), flip_seq(ctx_in)
        oc, s_ctx = chunk_gated_delta(*ctx_in, s_zero)
        o, _ = chunk_gated_delta(*lat_in, s_ctx)
        if d == 1:
            o, oc = flip_seq((o, oc))
        outs.append(o)
        outs_c.append(oc)
    y = _gdn_out(outs[0] + outs[1], z, g_out)
    yc = _gdn_out(outs_c[0] + outs_c[1], zc, g_out) if ctx_out else None
    return y, yc


def mixer_na(q, k, v, qc, kc, vc, rpb, ctx_out):
    B, S, _ = q.shape
    L = qc.shape[1]
    rows = S // GRID_W
    wr = min(NA_WIN_R, rows)
    scale = HEAD_DIM ** -0.5
    qg = q.reshape(B, rows, GRID_W, NA_HEADS, HEAD_DIM).transpose(1, 0, 3, 2, 4)
    kg = k.reshape(B, rows, GRID_W, NA_HEADS, HEAD_DIM).transpose(0, 3, 1, 2, 4)
    vg = v.reshape(B, rows, GRID_W, NA_HEADS, HEAD_DIM).transpose(0, 3, 1, 2, 4)
    kct = kc.reshape(B, L, NA_HEADS, HEAD_DIM).transpose(0, 2, 1, 3)
    vct = vc.reshape(B, L, NA_HEADS, HEAD_DIM).transpose(0, 2, 1, 3)
    cstart = np.clip(np.arange(GRID_W) - NA_WIN_C // 2, 0, GRID_W - NA_WIN_C)
    cidx = cstart[:, None] + np.arange(NA_WIN_C)[None, :]
    cbias = cidx - np.arange(GRID_W)[:, None] + NA_WIN_C - 1
    n_nb = wr * NA_WIN_C

    def row_fn(args):
        r, q_row = args
        rs = jnp.clip(r - wr // 2, 0, rows - wr)
        kb = lax.dynamic_slice_in_dim(kg, rs, wr, axis=2)[:, :, :, cidx]
        vb = lax.dynamic_slice_in_dim(vg, rs, wr, axis=2)[:, :, :, cidx]
        ridx = rs + jnp.arange(wr) - r + NA_WIN_R - 1
        bias = rpb[:, ridx[None, :, None], cbias[:, None, :]]
        s_nb = jnp.einsum('bhcd,bhwcjd->bhcwj', q_row, kb) * scale + bias[None]
        s_c = jnp.einsum('bhcd,bhld->bhcl', q_row, kct) * scale
        s = jnp.concatenate([s_nb.reshape(B, NA_HEADS, GRID_W, n_nb), s_c.astype(s_nb.dtype)], axis=-1)
        p = jax.nn.softmax(s.astype(jnp.float32), axis=-1).astype(v.dtype)
        return (jnp.einsum('bhcwj,bhwcjd->bhcd', p[..., :n_nb].reshape(B, NA_HEADS, GRID_W, wr, NA_WIN_C), vb)
                + jnp.einsum('bhcl,bhld->bhcd', p[..., n_nb:], vct))

    o = lax.map(row_fn, (jnp.arange(rows), qg))
    o = o.transpose(1, 0, 3, 2, 4).reshape(B, S, NA_HEADS * HEAD_DIM)
    oc = None
    if ctx_out:
        oc = gqa_dense(qc.reshape(B, L, NA_HEADS, 1, HEAD_DIM), kc.reshape(B, L, NA_HEADS, HEAD_DIM),
                       vc.reshape(B, L, NA_HEADS, HEAD_DIM)).reshape(B, L, NA_HEADS * HEAD_DIM)
    return o, oc


def merge_branches(ys, gate_logits, w_branch, w_out):
    gates = jax.nn.sigmoid(gate_logits.astype(jnp.float32))
    m = gates[..., :D_MODEL] * (ys[0] @ w_branch[0]).astype(jnp.float32)
    for i in range(1, N_BRANCH):
        m = m + gates[..., i * D_MODEL:(i + 1) * D_MODEL] * (ys[i] @ w_branch[i]).astype(jnp.float32)
    return m.astype(ys[0].dtype) @ w_out


def moe_ffn(t, w_router, b_router, w_gu, b_gu, w_dn, b_dn):
    logits = (t @ w_router + b_router).astype(jnp.float32)
    top_v, top_i = lax.top_k(logits, TOP_K)
    top_w = jax.nn.softmax(top_v, axis=-1)
    gate_w = jnp.sum(jax.nn.one_hot(top_i, N_EXPERTS, dtype=jnp.float32) * top_w[..., None], axis=1)
    out = jnp.zeros(t.shape, jnp.float32)
    for e in range(N_EXPERTS):
        gu = t @ w_gu[e] + b_gu[e]
        gate = jnp.minimum(gu[:, :D_EXPERT], SWIGLU_LIMIT)
        up = jnp.clip(gu[:, D_EXPERT:], -SWIGLU_LIMIT, SWIGLU_LIMIT)
        act = (up + 1) * gate * jax.nn.sigmoid(SWIGLU_ALPHA * gate)
        out = out + gate_w[:, e:e + 1] * (act @ w_dn[e] + b_dn[e]).astype(jnp.float32)
    return out.astype(t.dtype)


def trunk_layer(x, xc, mod, mod_c, g_norm1, g_norm2, w_in, a_gq, a_gk,
                b_conv_w, b_conv_b, b_lam, b_wr, b_br, b_wi, b_bi,
                c_conv_w, c_alog, c_dtb, c_gout, d_rpb, w_branch, w_out,
                w_router, b_router, w_gu, b_gu, w_dn, b_dn, cos, sin, ctx_out):
    sh1, sc1, gt1, sh2, sc2, gt2 = jnp.split(mod, 6, axis=-1)
    csh1, csc1, cgt1, csh2, csc2, cgt2 = jnp.split(mod_c, 6, axis=-1)
    h = modulate(rms_norm(x, g_norm1), sh1, sc1)
    hc = modulate(rms_norm(xc, g_norm1), csh1, csc1)
    u = split_in(h @ w_in)
    uc = split_in(hc @ w_in)
    ya, yac = mixer_gqa(u['a_q'], u['a_k'], u['a_v'], uc['a_q'], uc['a_k'], uc['a_v'],
                        a_gq, a_gk, cos, sin, ctx_out)
    yb, ybc = mixer_rglru(u['b_x'], u['b_y'], uc['b_x'], uc['b_y'], b_conv_w, b_conv_b,
                          b_lam, b_wr, b_br, b_wi, b_bi, ctx_out)
    yc, ycc = mixer_gdn(u['c_q'], u['c_k'], u['c_v'], u['c_z'], u['c_a'], u['c_b'],
                        uc['c_q'], uc['c_k'], uc['c_v'], uc['c_z'], uc['c_a'], uc['c_b'],
                        c_conv_w, c_alog, c_dtb, c_gout, ctx_out)
    yd, ydc = mixer_na(u['d_q'], u['d_k'], u['d_v'], uc['d_q'], uc['d_k'], uc['d_v'], d_rpb, ctx_out)
    x = x + gt1 * merge_branches((ya, yb, yc, yd), u['gate'], w_branch, w_out)
    h2 = modulate(rms_norm(x, g_norm2), sh2, sc2)
    B, S, D = x.shape
    if ctx_out:
        xc = xc + cgt1 * merge_branches((yac, ybc, ycc, ydc), uc['gate'], w_branch, w_out)
        hc2 = modulate(rms_norm(xc, g_norm2), csh2, csc2)
        f = moe_ffn(jnp.concatenate([h2.reshape(-1, D), hc2.reshape(-1, D)], axis=0),
                    w_router, b_router, w_gu, b_gu, w_dn, b_dn)
        x = x + gt2 * f[:B * S].reshape(B, S, D)
        xc = xc + cgt2 * f[B * S:].reshape(xc.shape)
    else:
        x = x + gt2 * moe_ffn(h2.reshape(-1, D), w_router, b_router, w_gu, b_gu, w_dn, b_dn).reshape(B, S, D)
    return x, xc


def setup_inputs(seed: int = 0) -> dict:
    key = jax.random.key(seed)
    ks = iter(jax.random.split(key, 40))
    D = D_MODEL
    f32 = jnp.float32

    def nrm(shape, scale):
        return jax.random.normal(next(ks), shape, f32) * scale

    def unif(shape, lo, hi):
        return jax.random.uniform(next(ks), shape, f32, minval=lo, maxval=hi)

    x = nrm((BATCH, SEQ, D), 1.0)
    c = nrm((BATCH, D), 1.0)
    ctx = nrm((BATCH, CTX_LEN, D), 1.0)
    c_ctx = nrm((D,), 1.0)
    w_mod = nrm((DEPTH, D, 6 * D), D ** -0.5)
    b_mod = nrm((DEPTH, 6 * D), 0.02)
    g_norm1 = 1.0 + nrm((DEPTH, D), 0.1)
    g_norm2 = 1.0 + nrm((DEPTH, D), 0.1)
    w_in = nrm((DEPTH, D, IN_WIDTH), D ** -0.5)
    a_gq = 1.0 + nrm((DEPTH, HEAD_DIM), 0.1)
    a_gk = 1.0 + nrm((DEPTH, HEAD_DIM), 0.1)
    b_conv_w = nrm((DEPTH, CONV_W, LRU_WIDTH), CONV_W ** -0.5)
    b_conv_b = nrm((DEPTH, LRU_WIDTH), 0.02)
    a0 = unif((DEPTH, 2, LRU_WIDTH), 0.9, 0.999) ** (1.0 / LRU_C)
    b_lam = jnp.log(a0) - jnp.log1p(-a0)
    b_wr = nrm((DEPTH, 2, LRU_BLOCKS, LRU_BLOCK_W, LRU_BLOCK_W), LRU_BLOCK_W ** -0.5)
    b_br = nrm((DEPTH, 2, LRU_WIDTH), 0.02)
    b_wi = nrm((DEPTH, 2, LRU_BLOCKS, LRU_BLOCK_W, LRU_BLOCK_W), LRU_BLOCK_W ** -0.5)
    b_bi = nrm((DEPTH, 2, LRU_WIDTH), 0.02)
    c_conv_w = nrm((DEPTH, CONV_W, DN_HEADS * (2 * DN_DK + DN_DV)), CONV_W ** -0.5)
    c_alog = jnp.log(unif((DEPTH, 2, DN_HEADS), 1.0, 16.0))
    dt = jnp.exp(unif((DEPTH, 2, DN_HEADS), float(np.log(1e-3)), float(np.log(1e-1))))
    c_dtb = dt + jnp.log(-jnp.expm1(-dt))
    c_gout = 1.0 + nrm((DEPTH, DN_DV), 0.1)
    d_rpb = nrm((DEPTH, NA_HEADS, 2 * NA_WIN_R - 1, 2 * NA_WIN_C - 1), 0.1)
    w_branch = nrm((DEPTH, N_BRANCH, BRANCH_W, D), BRANCH_W ** -0.5)
    w_out = nrm((DEPTH, D, D), D ** -0.5)
    w_router = nrm((DEPTH, D, N_EXPERTS), D ** -0.5)
    b_router = nrm((DEPTH, N_EXPERTS), 0.01)
    w_gu = nrm((DEPTH, N_EXPERTS, D, 2 * D_EXPERT), D ** -0.5)
    b_gu = nrm((DEPTH, N_EXPERTS, 2 * D_EXPERT), 0.02)
    w_dn = nrm((DEPTH, N_EXPERTS, D_EXPERT, D), D_EXPERT ** -0.5)
    b_dn = nrm((DEPTH, N_EXPERTS, D), 0.02)
    g_final = 1.0 + nrm((D,), 0.1)
    return {'x': x, 'c': c, 'ctx': ctx, 'c_ctx': c_ctx, 'w_mod': w_mod, 'b_mod': b_mod,
            'g_norm1': g_norm1, 'g_norm2': g_norm2, 'w_in': w_in, 'a_gq': a_gq, 'a_gk': a_gk,
            'b_conv_w': b_conv_w, 'b_conv_b': b_conv_b, 'b_lam': b_lam, 'b_wr': b_wr, 'b_br': b_br,
            'b_wi': b_wi, 'b_bi': b_bi, 'c_conv_w': c_conv_w, 'c_alog': c_alog, 'c_dtb': c_dtb,
            'c_gout': c_gout, 'd_rpb': d_rpb, 'w_branch': w_branch, 'w_out': w_out,
            'w_router': w_router, 'b_router': b_router, 'w_gu': w_gu, 'b_gu': b_gu,
            'w_dn': w_dn, 'b_dn': b_dn, 'g_final': g_final}


def reference(x, c, ctx, c_ctx, w_mod, b_mod, g_norm1, g_norm2, w_in, a_gq, a_gk,
              b_conv_w, b_conv_b, b_lam, b_wr, b_br, b_wi, b_bi, c_conv_w, c_alog, c_dtb,
              c_gout, d_rpb, w_branch, w_out, w_router, b_router, w_gu, b_gu, w_dn, b_dn, g_final):
    cos, sin = axial_rope_tables(x.shape[1])
    sc = jax.nn.silu(c)
    scc = jax.nn.silu(c_ctx)
    xc = ctx
    for l in range(DEPTH):
        mod = (sc @ w_mod[l] + b_mod[l])[:, None, :]
        mod_c = (scc @ w_mod[l] + b_mod[l])[None, None, :]
        x, xc = trunk_layer(x, xc, mod, mod_c, g_norm1[l], g_norm2[l], w_in[l], a_gq[l], a_gk[l],
                            b_conv_w[l], b_conv_b[l], b_lam[l], b_wr[l], b_br[l], b_wi[l], b_bi[l],
                            c_conv_w[l], c_alog[l], c_dtb[l], c_gout[l], d_rpb[l], w_branch[l], w_out[l],
                            w_router[l], b_router[l], w_gu[l], b_gu[l], w_dn[l], b_dn[l], cos, sin,
                            l < DEPTH - 1)
    return rms_norm(x, g_final)
```

```python
import functools

import jax
import jax.numpy as jnp
import numpy as np
from jax import lax
from jax.experimental import pallas as pl
from jax.experimental.pallas import tpu as pltpu

GRID_W = 64
HEAD_DIM = 64
EPS = 1e-6

A_HEADS = 8
A_KV_HEADS = 2
A_GROUP = A_HEADS // A_KV_HEADS
A_BLOCK = 128
ROPE_THETA = 10000.0

LRU_WIDTH = 512
LRU_BLOCKS = 8
LRU_BLOCK_W = LRU_WIDTH // LRU_BLOCKS
LRU_C = 8.0

CONV_W = 4
CONV_PAD = ((CONV_W - 1) // 2, CONV_W // 2)

DN_HEADS = 4
DN_DK = 128
DN_DV = 128
DN_CHUNK = 64

NA_HEADS = 8
NA_WIN_R = 8
NA_WIN_C = 16

N_BRANCH = 4
BRANCH_W = 512

TOP_K = 4
SWIGLU_LIMIT = 7.0
SWIGLU_ALPHA = 1.702

V7X_LANES = 128
V7X_MXU_DIM = 256
V7X_VMEM_BYTES = 64 * 1024 * 1024

TOKEN_TILE = 256
MOE_TILE = 256


def _in_splits(d_model):
    return (
        ('a_q', A_HEADS * HEAD_DIM), ('a_k', A_KV_HEADS * HEAD_DIM), ('a_v', A_KV_HEADS * HEAD_DIM),
        ('b_x', LRU_WIDTH), ('b_y', LRU_WIDTH),
        ('c_q', DN_HEADS * DN_DK), ('c_k', DN_HEADS * DN_DK), ('c_v', DN_HEADS * DN_DV),
        ('c_z', DN_HEADS * DN_DV), ('c_a', 2 * DN_HEADS), ('c_b', 2 * DN_HEADS),
        ('d_q', NA_HEADS * HEAD_DIM), ('d_k', NA_HEADS * HEAD_DIM), ('d_v', NA_HEADS * HEAD_DIM),
        ('gate', N_BRANCH * d_model),
    )


def _vmem_params(nbytes, semantics):
    limit = int(min(max(2 * nbytes, 32 * 1024 * 1024), V7X_VMEM_BYTES - 8 * 1024 * 1024))
    return pltpu.CompilerParams(dimension_semantics=semantics, vmem_limit_bytes=limit)


def _norm_mod_kernel(x_ref, g_ref, mod_ref, h_ref, *, shift_idx, scale_idx):
    x = x_ref[0]
    y = x * lax.rsqrt(jnp.mean(x * x, axis=-1, keepdims=True) + EPS) * g_ref[...]
    shift = mod_ref[0, 0, shift_idx:shift_idx + 1, :]
    scale = mod_ref[0, 0, scale_idx:scale_idx + 1, :]
    h_ref[0] = (y * (1 + scale) + shift).astype(h_ref.dtype)


def _norm_mod_router_kernel(x_ref, g_ref, mod_ref, wr_ref, br_ref, h_ref, lg_ref, *, shift_idx, scale_idx):
    x = x_ref[0]
    y = x * lax.rsqrt(jnp.mean(x * x, axis=-1, keepdims=True) + EPS) * g_ref[...]
    shift = mod_ref[0, 0, shift_idx:shift_idx + 1, :]
    scale = mod_ref[0, 0, scale_idx:scale_idx + 1, :]
    h = y * (1 + scale) + shift
    h_ref[0] = h.astype(h_ref.dtype)
    lg_ref[0] = jnp.dot(h, wr_ref[...], preferred_element_type=jnp.float32,
                        precision=lax.Precision.HIGHEST) + br_ref[...]


def _norm_mod(xa, g, mod_all, n_lat_tiles, shift_idx, scale_idx, router=None):
    B, NTOK, D = xa.shape
    tm = TOKEN_TILE
    grid = (B, NTOK // tm)
    x_spec = pl.BlockSpec((1, tm, D), lambda b, t: (b, t, 0))
    g_spec = pl.BlockSpec((1, D), lambda b, t: (0, 0))
    mod_spec = pl.BlockSpec((1, 1, 6, D), lambda b, t: (b, jnp.where(t >= n_lat_tiles, 1, 0), 0, 0))
    h_spec = pl.BlockSpec((1, tm, D), lambda b, t: (b, t, 0))
    params = _vmem_params(8 * tm * D * 4, ("parallel", "parallel"))
    if router is None:
        return pl.pallas_call(
            functools.partial(_norm_mod_kernel, shift_idx=shift_idx, scale_idx=scale_idx),
            grid=grid, in_specs=[x_spec, g_spec, mod_spec], out_specs=h_spec,
            out_shape=jax.ShapeDtypeStruct((B, NTOK, D), jnp.bfloat16),
            compiler_params=params, name="norm_mod")(xa, g.reshape(1, D), mod_all)
    w_router, b_router = router
    E = w_router.shape[1]
    return pl.pallas_call(
        functools.partial(_norm_mod_router_kernel, shift_idx=shift_idx, scale_idx=scale_idx),
        grid=grid,
        in_specs=[x_spec, g_spec, mod_spec, pl.BlockSpec((D, E), lambda b, t: (0, 0)),
                  pl.BlockSpec((1, E), lambda b, t: (0, 0))],
        out_specs=[h_spec, pl.BlockSpec((1, tm, E), lambda b, t: (b, t, 0))],
        out_shape=[jax.ShapeDtypeStruct((B, NTOK, D), jnp.bfloat16),
                   jax.ShapeDtypeStruct((B, NTOK, E), jnp.float32)],
        compiler_params=params, name="norm_mod_router")(xa, g.reshape(1, D), mod_all, w_router,
                                                         b_router.reshape(1, E))


def _mm_kernel(a_ref, w_ref, o_ref):
    o_ref[...] = jnp.dot(a_ref[...], w_ref[...], preferred_element_type=jnp.float32).astype(o_ref.dtype)


def _matmul(a, w, out_dtype, tm=512, tn=512):
    M, K = a.shape
    N = w.shape[1]
    assert M % tm == 0 and N % tn == 0, (M, N, tm, tn)
    nbytes = 2 * (tm * K * a.dtype.itemsize + K * tn * w.dtype.itemsize + tm * tn * 4)
    return pl.pallas_call(
        _mm_kernel, grid=(N // tn, M // tm),
        in_specs=[pl.BlockSpec((tm, K), lambda j, i: (i, 0)), pl.BlockSpec((K, tn), lambda j, i: (0, j))],
        out_specs=pl.BlockSpec((tm, tn), lambda j, i: (i, j)),
        out_shape=jax.ShapeDtypeStruct((M, N), out_dtype),
        compiler_params=_vmem_params(nbytes, ("parallel", "parallel")), name="matmul")(a, w)


def _merge_kernel(ya_ref, yb_ref, yc_ref, yd_ref, gl_ref, wb_ref, wo_ref, x_ref, mod_ref, o_ref, *, d_model):
    m = None
    for i, y_ref in enumerate((ya_ref, yb_ref, yc_ref, yd_ref)):
        gates = jax.nn.sigmoid(gl_ref[0, :, i * d_model:(i + 1) * d_model])
        p = gates * jnp.dot(y_ref[0], wb_ref[i], preferred_element_type=jnp.float32)
        m = p if m is None else m + p
    out = jnp.dot(m.astype(jnp.bfloat16), wo_ref[...], preferred_element_type=jnp.float32)
    o_ref[0] = x_ref[0] + mod_ref[0, 0, 2:3, :] * out


def _merge(ys, gate_logits, w_branch, w_out, xa, mod_all, n_lat_tiles):
    B, NTOK, D = xa.shape
    tm = TOKEN_TILE
    tok = lambda w: pl.BlockSpec((1, tm, w), lambda b, t: (b, t, 0))
    nbytes = (2 * (4 * tm * BRANCH_W * 2 + tm * 4 * D * 4 + 2 * tm * D * 4)
              + 2 * (N_BRANCH * BRANCH_W * D * 2 + D * D * 2) + 4 * tm * D * 4)
    return pl.pallas_call(
        functools.partial(_merge_kernel, d_model=D), grid=(B, NTOK // tm),
        in_specs=[tok(BRANCH_W)] * 4 + [
            tok(N_BRANCH * D),
            pl.BlockSpec((N_BRANCH, BRANCH_W, D), lambda b, t: (0, 0, 0)),
            pl.BlockSpec((D, D), lambda b, t: (0, 0)),
            tok(D),
            pl.BlockSpec((1, 1, 6, D), lambda b, t: (b, jnp.where(t >= n_lat_tiles, 1, 0), 0, 0))],
        out_specs=tok(D), out_shape=jax.ShapeDtypeStruct((B, NTOK, D), jnp.float32),
        compiler_params=_vmem_params(nbytes, ("parallel", "parallel")), name="merge")(
            *ys, gate_logits, w_branch, w_out, xa, mod_all)


def _moe_kernel(te_ref, nv_ref, x_ref, wgu_ref, bgu_ref, wdn_ref, bdn_ref, o_ref, *, d_expert):
    i = pl.program_id(0)

    @pl.when(i < nv_ref[0])
    def _():
        gu = jnp.dot(x_ref[...], wgu_ref[0], preferred_element_type=jnp.float32) + bgu_ref[0]
        gate = jnp.minimum(gu[:, :d_expert], SWIGLU_LIMIT)
        up = jnp.clip(gu[:, d_expert:], -SWIGLU_LIMIT, SWIGLU_LIMIT)
        act = (up + 1) * gate * jax.nn.sigmoid(SWIGLU_ALPHA * gate)
        y = jnp.dot(act.astype(jnp.bfloat16), wdn_ref[0], preferred_element_type=jnp.float32) + bdn_ref[0]
        o_ref[...] = y.astype(o_ref.dtype)

    @pl.when(i >= nv_ref[0])
    def _():
        o_ref[...] = jnp.zeros_like(o_ref)


def _moe_experts(x_sorted, tile_expert, n_valid, w_gu, b_gu, w_dn, b_dn):
    P, D = x_sorted.shape
    E, _, DE2 = w_gu.shape
    DE = DE2 // 2
    tm = MOE_TILE
    grid_spec = pltpu.PrefetchScalarGridSpec(
        num_scalar_prefetch=2, grid=(P // tm,),
        in_specs=[pl.BlockSpec((tm, D), lambda i, te, nv: (i, 0)),
                  pl.BlockSpec((1, D, DE2), lambda i, te, nv: (te[i], 0, 0)),
                  pl.BlockSpec((1, 1, DE2), lambda i, te, nv: (te[i], 0, 0)),
                  pl.BlockSpec((1, DE, D), lambda i, te, nv: (te[i], 0, 0)),
                  pl.BlockSpec((1, 1, D), lambda i, te, nv: (te[i], 0, 0))],
        out_specs=pl.BlockSpec((tm, D), lambda i, te, nv: (i, 0)))
    nbytes = 2 * (tm * D * 2 + D * DE2 * 2 + DE * D * 2 + tm * D * 4) + tm * DE2 * 4 * 3
    return pl.pallas_call(
        functools.partial(_moe_kernel, d_expert=DE), grid_spec=grid_spec,
        out_shape=jax.ShapeDtypeStruct((P, D), jnp.float32),
        compiler_params=_vmem_params(nbytes, ("arbitrary",)), name="moe_experts")(
            tile_expert, n_valid, x_sorted, w_gu, b_gu.reshape(E, 1, DE2), w_dn, b_dn.reshape(E, 1, D))


def _moe(h2, logits, w_gu, b_gu, w_dn, b_dn):
    T, D = h2.shape
    E = logits.shape[1]
    tm = MOE_TILE
    top_v, top_i = lax.top_k(logits, TOP_K)
    top_w = jax.nn.softmax(top_v, axis=-1)
    flat_e = top_i.reshape(-1).astype(jnp.int32)
    n_pairs = T * TOP_K
    counts = jnp.sum(jax.nn.one_hot(flat_e, E, dtype=jnp.int32), axis=0)
    padded = ((counts + tm - 1) // tm) * tm
    pad_end = jnp.cumsum(padded)
    pad_start = pad_end - padded
    start = jnp.cumsum(counts) - counts
    order = jnp.argsort(flat_e, stable=True)
    rank_sorted = jnp.arange(n_pairs, dtype=jnp.int32) - start[flat_e[order]]
    dest_sorted = pad_start[flat_e[order]] + rank_sorted
    P = n_pairs + E * tm
    slot_token = jnp.zeros((P,), jnp.int32).at[dest_sorted].set((order // TOP_K).astype(jnp.int32))
    dest = jnp.zeros((n_pairs,), jnp.int32).at[order].set(dest_sorted)
    n_valid = (pad_end[-1] // tm).astype(jnp.int32).reshape(1)
    tile_start = jnp.arange(P // tm, dtype=jnp.int32) * tm
    tile_expert = jnp.minimum(jnp.searchsorted(pad_end, tile_start, side='right'), E - 1).astype(jnp.int32)
    last_expert = tile_expert[jnp.maximum(n_valid[0] - 1, 0)]
    tile_expert = jnp.where(tile_start < pad_end[-1], tile_expert, last_expert)
    x_sorted = jnp.take(h2, slot_token, axis=0)
    y_sorted = _moe_experts(x_sorted, tile_expert, n_valid, w_gu, b_gu, w_dn, b_dn)
    y = jnp.take(y_sorted, dest, axis=0).reshape(T, TOP_K, D)
    return jnp.sum(y * top_w[..., None], axis=1)


def _final_norm_kernel(x_ref, g_ref, o_ref):
    x = x_ref[0]
    o_ref[0] = x * lax.rsqrt(jnp.mean(x * x, axis=-1, keepdims=True) + EPS) * g_ref[...]


def _final_norm(xa, g, seq):
    B, _, D = xa.shape
    tm = TOKEN_TILE
    return pl.pallas_call(
        _final_norm_kernel, grid=(B, seq // tm),
        in_specs=[pl.BlockSpec((1, tm, D), lambda b, t: (b, t, 0)), pl.BlockSpec((1, D), lambda b, t: (0, 0))],
        out_specs=pl.BlockSpec((1, tm, D), lambda b, t: (b, t, 0)),
        out_shape=jax.ShapeDtypeStruct((B, seq, D), jnp.float32),
        compiler_params=_vmem_params(4 * tm * D * 4, ("parallel", "parallel")), name="final_norm")(
            xa, g.reshape(1, D))


def _rms_norm(x, g):
    xf = x.astype(jnp.float32)
    y = xf * lax.rsqrt(jnp.mean(xf * xf, axis=-1, keepdims=True) + EPS)
    return (y * g.astype(jnp.float32)).astype(x.dtype)


def _l2_normalize(x):
    xf = x.astype(jnp.float32)
    return xf * lax.rsqrt(jnp.sum(xf * xf, axis=-1, keepdims=True) + EPS)


def _flip_seq(ts):
    return tuple(jnp.flip(t, axis=1) for t in ts)


def _dwconv(x, w, b=None):
    y = lax.conv_general_dilated(x, w[:, None, :].astype(x.dtype), window_strides=(1,),
                                 padding=[CONV_PAD], dimension_numbers=('NWC', 'WIO', 'NWC'),
                                 feature_group_count=x.shape[-1])
    return y if b is None else y + b.astype(x.dtype)


def _axial_rope_tables(n_tokens):
    t = jnp.arange(n_tokens)
    row = (t // GRID_W).astype(jnp.float32)
    col = (t % GRID_W).astype(jnp.float32)
    n_freq = HEAD_DIM // 4
    inv = ROPE_THETA ** (-jnp.arange(n_freq, dtype=jnp.float32) / n_freq)
    ang = jnp.concatenate([row[:, None] * inv[None], col[:, None] * inv[None]], axis=-1)
    return jnp.cos(ang), jnp.sin(ang)


def _apply_rope(x, cos, sin):
    half = x.shape[-1] // 2
    shp = (x.shape[1],) + (1,) * (x.ndim - 3) + (half,)
    cos, sin = cos.reshape(shp), sin.reshape(shp)
    xf = x.astype(jnp.float32)
    x1, x2 = xf[..., :half], xf[..., half:]
    return jnp.concatenate([x1 * cos - x2 * sin, x2 * cos + x1 * sin], axis=-1).astype(x.dtype)


def _gqa_dense(q, k, v):
    s = jnp.einsum('bqkgd,blkd->bkgql', q, k) * (q.shape[-1] ** -0.5)
    p = jax.nn.softmax(s.astype(jnp.float32), axis=-1).astype(v.dtype)
    return jnp.einsum('bkgql,blkd->bqkgd', p, v)


def _mixer_gqa(q, k, v, qc, kc, vc, g_q, g_k, cos, sin):
    B, S, _ = q.shape
    L = qc.shape[1]
    q = q.reshape(B, S, A_KV_HEADS, A_GROUP, HEAD_DIM)
    k = k.reshape(B, S, A_KV_HEADS, HEAD_DIM)
    v = v.reshape(B, S, A_KV_HEADS, HEAD_DIM)
    qc = _rms_norm(qc.reshape(B, L, A_KV_HEADS, A_GROUP, HEAD_DIM), g_q)
    kc = _rms_norm(kc.reshape(B, L, A_KV_HEADS, HEAD_DIM), g_k)
    vc = vc.reshape(B, L, A_KV_HEADS, HEAD_DIM)
    q = _apply_rope(_rms_norm(q, g_q), cos, sin)
    k = _apply_rope(_rms_norm(k, g_k), cos, sin)
    scale = HEAD_DIM ** -0.5

    def block(qb):
        s = jnp.concatenate([jnp.einsum('bqkgd,bskd->bkgqs', qb, k),
                             jnp.einsum('bqkgd,blkd->bkgql', qb, kc)], axis=-1) * scale
        p = jax.nn.softmax(s.astype(jnp.float32), axis=-1).astype(v.dtype)
        return (jnp.einsum('bkgqs,bskd->bqkgd', p[..., :S], v)
                + jnp.einsum('bkgql,blkd->bqkgd', p[..., S:], vc))

    qb = q.reshape(B, S // A_BLOCK, A_BLOCK, A_KV_HEADS, A_GROUP, HEAD_DIM).swapaxes(0, 1)
    o = lax.map(block, qb).swapaxes(0, 1).reshape(B, S, A_HEADS * HEAD_DIM)
    oc = _gqa_dense(qc, kc, vc).reshape(B, L, A_HEADS * HEAD_DIM)
    return o, oc


def _rglru_coeffs(x, lam, w_r, b_r, w_i, b_i):
    xf = x.astype(jnp.float32)
    blk = xf.reshape(x.shape[:-1] + (LRU_BLOCKS, LRU_BLOCK_W))
    r = jax.nn.sigmoid(jnp.einsum('blnc,ncd->blnd', blk, w_r).reshape(x.shape) + b_r)
    i = jax.nn.sigmoid(jnp.einsum('blnc,ncd->blnd', blk, w_i).reshape(x.shape) + b_i)
    log_a = -LRU_C * r * jax.nn.softplus(-lam.astype(jnp.float32))
    return jnp.exp(log_a), jnp.sqrt(-jnp.expm1(2 * log_a)) * (i * xf)


def _lin_combine(e1, e2):
    a1, b1 = e1
    a2, b2 = e2
    return a1 * a2, a2 * b1 + b2


def _linear_scan(a, b, h0):
    b = b.at[:, 0].add(a[:, 0] * h0)
    _, h = lax.associative_scan(_lin_combine, (a, b), axis=1)
    return h


def _mixer_rglru(xb, yb, xbc, ybc, conv_w, conv_b, lam, w_r, b_r, w_i, b_i):
    xb = _dwconv(xb, conv_w, conv_b)
    xbc = _dwconv(xbc, conv_w, conv_b)
    h0 = jnp.zeros((xb.shape[0], LRU_WIDTH), jnp.float32)
    hs, hcs = [], []
    for d in range(2):
        a, bb = _rglru_coeffs(xb, lam[d], w_r[d], b_r[d], w_i[d], b_i[d])
        ac, bc = _rglru_coeffs(xbc, lam[d], w_r[d], b_r[d], w_i[d], b_i[d])
        if d == 1:
            a, bb, ac, bc = _flip_seq((a, bb, ac, bc))
        hc = _linear_scan(ac, bc, h0)
        h = _linear_scan(a, bb, hc[:, -1])
        if d == 1:
            h, hc = _flip_seq((h, hc))
        hs.append(h)
        hcs.append(hc)
    y = ((hs[0] + hs[1]) * jax.nn.gelu(yb.astype(jnp.float32))).astype(yb.dtype)
    yc = ((hcs[0] + hcs[1]) * jax.nn.gelu(ybc.astype(jnp.float32))).astype(ybc.dtype)
    return y, yc


def _chunk_gated_delta(q, k, v, g, beta, s0):
    B, L, H, _ = q.shape
    C = DN_CHUNK
    n = L // C

    def blk(t):
        return jnp.moveaxis(t.reshape((B, n, C, H) + t.shape[3:]), 3, 1)

    q, k, v, beta = blk(q), blk(k), blk(v), blk(beta)
    gc = jnp.cumsum(blk(g), axis=-1)
    kb = k * beta[..., None]
    vb = v * beta[..., None]
    lower = np.tril(np.ones((C, C), dtype=bool))
    diff = gc[..., :, None] - gc[..., None, :]
    decay = jnp.where(lower, jnp.exp(jnp.where(lower, diff, 0.0)), 0.0)
    lmat = jnp.einsum('bhncd,bhned->bhnce', kb, k) * decay
    u = lax.linalg.triangular_solve(lmat, vb, left_side=True, lower=True, unit_diagonal=True)
    w = lax.linalg.triangular_solve(lmat, kb * jnp.exp(gc)[..., None], left_side=True, lower=True,
                                    unit_diagonal=True)
    qk = jnp.einsum('bhncd,bhned->bhnce', q, k) * decay
    qg = q * jnp.exp(gc)[..., None]
    kd = k * jnp.exp(gc[..., -1:] - gc)[..., None]
    glast = jnp.exp(gc[..., -1])

    def step(s, xs):
        qk_i, qg_i, w_i, u_i, kd_i, gl_i = xs
        vn = u_i - jnp.einsum('bhcd,bhde->bhce', w_i, s)
        o = jnp.einsum('bhcd,bhde->bhce', qg_i, s) + jnp.einsum('bhce,bhef->bhcf', qk_i, vn)
        s = s * gl_i[..., None, None] + jnp.einsum('bhcd,bhce->bhde', kd_i, vn)
        return s, o

    xs = tuple(jnp.moveaxis(t, 2, 0) for t in (qk, qg, w, u, kd, glast))
    s_fin, o = lax.scan(step, s0, xs)
    o = jnp.moveaxis(jnp.moveaxis(o, 0, 2), 1, 3).reshape(B, L, H, DN_DV)
    return o, s_fin


def _gdn_prep(q, k, v, ga, gb, conv_w, a_log, dt_bias):
    B, L, _ = q.shape
    nk = DN_HEADS * DN_DK
    qkv = jax.nn.silu(_dwconv(jnp.concatenate([q, k, v], axis=-1), conv_w).astype(jnp.float32))
    qh = _l2_normalize(qkv[..., :nk].reshape(B, L, DN_HEADS, DN_DK)) * (DN_DK ** -0.5)
    kh = _l2_normalize(qkv[..., nk:2 * nk].reshape(B, L, DN_HEADS, DN_DK))
    vh = qkv[..., 2 * nk:].reshape(B, L, DN_HEADS, DN_DV)
    g = -jnp.exp(a_log.astype(jnp.float32)) * jax.nn.softplus(
        ga.astype(jnp.float32).reshape(B, L, 2, DN_HEADS) + dt_bias.astype(jnp.float32))
    beta = jax.nn.sigmoid(gb.astype(jnp.float32).reshape(B, L, 2, DN_HEADS))
    return qh, kh, vh, g, beta


def _gdn_out(o, z, g_out):
    B, L, _ = z.shape
    zf = jax.nn.silu(z.astype(jnp.float32).reshape(B, L, DN_HEADS, DN_DV))
    return (_rms_norm(o, g_out) * zf).reshape(B, L, DN_HEADS * DN_DV).astype(z.dtype)


def _mixer_gdn(q, k, v, z, ga, gb, qc, kc, vc, zc, gac, gbc, conv_w, a_log, dt_bias, g_out):
    qh, kh, vh, g, beta = _gdn_prep(q, k, v, ga, gb, conv_w, a_log, dt_bias)
    qch, kch, vch, gcx, bcx = _gdn_prep(qc, kc, vc, gac, gbc, conv_w, a_log, dt_bias)
    s_zero = jnp.zeros((q.shape[0], DN_HEADS, DN_DK, DN_DV), jnp.float32)
    outs, outs_c = [], []
    for d in range(2):
        lat_in = (qh, kh, vh, g[:, :, d], beta[:, :, d])
        ctx_in = (qch, kch, vch, gcx[:, :, d], bcx[:, :, d])
        if d == 1:
            lat_in, ctx_in = _flip_seq(lat_in), _flip_seq(ctx_in)
        oc, s_ctx = _chunk_gated_delta(*ctx_in, s_zero)
        o, _ = _chunk_gated_delta(*lat_in, s_ctx)
        if d == 1:
            o, oc = _flip_seq((o, oc))
        outs.append(o)
        outs_c.append(oc)
    y = _gdn_out(outs[0] + outs[1], z, g_out)
    yc = _gdn_out(outs_c[0] + outs_c[1], zc, g_out)
    return y, yc


def _mixer_na(q, k, v, qc, kc, vc, rpb):
    B, S, _ = q.shape
    L = qc.shape[1]
    rows = S // GRID_W
    wr = min(NA_WIN_R, rows)
    scale = HEAD_DIM ** -0.5
    qg = q.reshape(B, rows, GRID_W, NA_HEADS, HEAD_DIM).transpose(1, 0, 3, 2, 4)
    kg = k.reshape(B, rows, GRID_W, NA_HEADS, HEAD_DIM).transpose(0, 3, 1, 2, 4)
    vg = v.reshape(B, rows, GRID_W, NA_HEADS, HEAD_DIM).transpose(0, 3, 1, 2, 4)
    kct = kc.reshape(B, L, NA_HEADS, HEAD_DIM).transpose(0, 2, 1, 3)
    vct = vc.reshape(B, L, NA_HEADS, HEAD_DIM).transpose(0, 2, 1, 3)
    cstart = np.clip(np.arange(GRID_W) - NA_WIN_C // 2, 0, GRID_W - NA_WIN_C)
    cidx = cstart[:, None] + np.arange(NA_WIN_C)[None, :]
    cbias = cidx - np.arange(GRID_W)[:, None] + NA_WIN_C - 1
    n_nb = wr * NA_WIN_C

    def row_fn(args):
        r, q_row = args
        rs = jnp.clip(r - wr // 2, 0, rows - wr)
        kb = lax.dynamic_slice_in_dim(kg, rs, wr, axis=2)[:, :, :, cidx]
        vb = lax.dynamic_slice_in_dim(vg, rs, wr, axis=2)[:, :, :, cidx]
        ridx = rs + jnp.arange(wr) - r + NA_WIN_R - 1
        bias = rpb[:, ridx[None, :, None], cbias[:, None, :]]
        s_nb = jnp.einsum('bhcd,bhwcjd->bhcwj', q_row, kb) * scale + bias[None]
        s_c = jnp.einsum('bhcd,bhld->bhcl', q_row, kct) * scale
        s = jnp.concatenate([s_nb.reshape(B, NA_HEADS, GRID_W, n_nb), s_c.astype(s_nb.dtype)], axis=-1)
        p = jax.nn.softmax(s.astype(jnp.float32), axis=-1).astype(v.dtype)
        return (jnp.einsum('bhcwj,bhwcjd->bhcd', p[..., :n_nb].reshape(B, NA_HEADS, GRID_W, wr, NA_WIN_C), vb)
                + jnp.einsum('bhcl,bhld->bhcd', p[..., n_nb:], vct))

    o = lax.map(row_fn, (jnp.arange(rows), qg))
    o = o.transpose(1, 0, 3, 2, 4).reshape(B, S, NA_HEADS * HEAD_DIM)
    oc = _gqa_dense(qc.reshape(B, L, NA_HEADS, 1, HEAD_DIM), kc.reshape(B, L, NA_HEADS, HEAD_DIM),
                    vc.reshape(B, L, NA_HEADS, HEAD_DIM)).reshape(B, L, NA_HEADS * HEAD_DIM)
    return o, oc


def kernel(x, c, ctx, c_ctx, w_mod, b_mod, g_norm1, g_norm2, w_in, a_gq, a_gk, b_conv_w, b_conv_b, b_lam,
           b_wr, b_br, b_wi, b_bi, c_conv_w, c_alog, c_dtb, c_gout, d_rpb, w_branch, w_out, w_router,
           b_router, w_gu, b_gu, w_dn, b_dn, g_final):
    B, S, D = x.shape
    L = ctx.shape[1]
    depth = w_mod.shape[0]
    NTOK = S + L
    assert S % TOKEN_TILE == 0 and L % TOKEN_TILE == 0
    n_lat_tiles = S // TOKEN_TILE
    splits = _in_splits(D)
    in_width = sum(w for _, w in splits)
    offs = np.cumsum([w for _, w in splits])[:-1].tolist()
    names = [n for n, _ in splits]
    in_pad = -in_width % 512
    bf16 = jnp.bfloat16

    cos, sin = _axial_rope_tables(S)
    sc = jax.nn.silu(c)
    scc = jax.nn.silu(c_ctx)
    xa = jnp.concatenate([x, ctx], axis=1)

    for l in range(depth):
        mod = sc @ w_mod[l] + b_mod[l]
        mod_c = scc @ w_mod[l] + b_mod[l]
        mod_all = jnp.stack([mod.reshape(B, 6, D), jnp.broadcast_to(mod_c.reshape(1, 6, D), (B, 6, D))], axis=1)

        h = _norm_mod(xa, g_norm1[l], mod_all, n_lat_tiles, 0, 1)
        w_in_l = jnp.pad(w_in[l], ((0, 0), (0, in_pad))).astype(bf16)
        u_all = _matmul(h.reshape(B * NTOK, D), w_in_l, jnp.float32)[:, :in_width].reshape(B, NTOK, in_width)
        u = dict(zip(names, jnp.split(u_all[:, :S], offs, axis=-1)))
        uc = dict(zip(names, jnp.split(u_all[:, S:], offs, axis=-1)))

        ya, yac = _mixer_gqa(u['a_q'], u['a_k'], u['a_v'], uc['a_q'], uc['a_k'], uc['a_v'],
                             a_gq[l], a_gk[l], cos, sin)
        yb, ybc = _mixer_rglru(u['b_x'], u['b_y'], uc['b_x'], uc['b_y'], b_conv_w[l], b_conv_b[l],
                               b_lam[l], b_wr[l], b_br[l], b_wi[l], b_bi[l])
        yc, ycc = _mixer_gdn(u['c_q'], u['c_k'], u['c_v'], u['c_z'], u['c_a'], u['c_b'],
                             uc['c_q'], uc['c_k'], uc['c_v'], uc['c_z'], uc['c_a'], uc['c_b'],
                             c_conv_w[l], c_alog[l], c_dtb[l], c_gout[l])
        yd, ydc = _mixer_na(u['d_q'], u['d_k'], u['d_v'], uc['d_q'], uc['d_k'], uc['d_v'], d_rpb[l])
        ys = [jnp.concatenate([a, b], axis=1).astype(bf16)
              for a, b in ((ya, yac), (yb, ybc), (yc, ycc), (yd, ydc))]
        gate_logits = jnp.concatenate([u['gate'], uc['gate']], axis=1)

        xa = _merge(ys, gate_logits, w_branch[l].astype(bf16), w_out[l].astype(bf16), xa, mod_all, n_lat_tiles)
        h2, logits = _norm_mod(xa, g_norm2[l], mod_all, n_lat_tiles, 3, 4, router=(w_router[l], b_router[l]))
        f = _moe(h2.reshape(B * NTOK, D), logits.reshape(B * NTOK, -1),
                 w_gu[l].astype(bf16), b_gu[l], w_dn[l].astype(bf16), b_dn[l])
        gt2 = mod_all[:, :, 5]
        gt2_rows = jnp.concatenate([jnp.broadcast_to(gt2[:, 0:1], (B, S, D)),
                                    jnp.broadcast_to(gt2[:, 1:2], (B, L, D))], axis=1)
        xa = xa + gt2_rows * f.reshape(B, NTOK, D)

    return _final_norm(xa, g_final, S)
```

```python
import functools

import jax
import jax.numpy as jnp
import numpy as np
from jax import lax
from jax.experimental import pallas as pl
from jax.experimental.pallas import tpu as pltpu

GRID_W = 64
HEAD_DIM = 64
EPS = 1e-6

A_HEADS = 8
A_KV_HEADS = 2
A_GROUP = A_HEADS // A_KV_HEADS
A_BLOCK = 128
ROPE_THETA = 10000.0

LRU_WIDTH = 512
LRU_BLOCKS = 8
LRU_BLOCK_W = LRU_WIDTH // LRU_BLOCKS
LRU_C = 8.0

CONV_W = 4
CONV_PAD = ((CONV_W - 1) // 2, CONV_W // 2)

DN_HEADS = 4
DN_DK = 128
DN_DV = 128
DN_CHUNK = 64

NA_HEADS = 8
NA_WIN_R = 8
NA_WIN_C = 16

N_BRANCH = 4
BRANCH_W = 512

TOP_K = 4
SWIGLU_LIMIT = 7.0
SWIGLU_ALPHA = 1.702

V7X_LANES = 128
V7X_MXU_DIM = 256
V7X_VMEM_BYTES = 64 * 1024 * 1024

TOKEN_TILE = 256
MOE_TILE = 256


def _in_splits(d_model):
    return (
        ('a_q', A_HEADS * HEAD_DIM), ('a_k', A_KV_HEADS * HEAD_DIM), ('a_v', A_KV_HEADS * HEAD_DIM),
        ('b_x', LRU_WIDTH), ('b_y', LRU_WIDTH),
        ('c_q', DN_HEADS * DN_DK), ('c_k', DN_HEADS * DN_DK), ('c_v', DN_HEADS * DN_DV),
        ('c_z', DN_HEADS * DN_DV), ('c_a', 2 * DN_HEADS), ('c_b', 2 * DN_HEADS),
        ('d_q', NA_HEADS * HEAD_DIM), ('d_k', NA_HEADS * HEAD_DIM), ('d_v', NA_HEADS * HEAD_DIM),
        ('gate', N_BRANCH * d_model),
    )


def _vmem_params(nbytes, semantics):
    limit = int(min(max(2 * nbytes, 32 * 1024 * 1024), V7X_VMEM_BYTES - 8 * 1024 * 1024))
    return pltpu.CompilerParams(dimension_semantics=semantics, vmem_limit_bytes=limit)


def _norm_mod_kernel(x_ref, g_ref, mod_ref, h_ref, *, shift_idx, scale_idx):
    x = x_ref[0]
    y = x * lax.rsqrt(jnp.mean(x * x, axis=-1, keepdims=True) + EPS) * g_ref[...]
    shift = mod_ref[0, 0, shift_idx:shift_idx + 1, :]
    scale = mod_ref[0, 0, scale_idx:scale_idx + 1, :]
    h_ref[0] = (y * (1 + scale) + shift).astype(h_ref.dtype)


def _norm_mod_router_kernel(x_ref, g_ref, mod_ref, wr_ref, br_ref, h_ref, lg_ref, *, shift_idx, scale_idx):
    x = x_ref[0]
    y = x * lax.rsqrt(jnp.mean(x * x, axis=-1, keepdims=True) + EPS) * g_ref[...]
    shift = mod_ref[0, 0, shift_idx:shift_idx + 1, :]
    scale = mod_ref[0, 0, scale_idx:scale_idx + 1, :]
    h = y * (1 + scale) + shift
    h_ref[0] = h.astype(h_ref.dtype)
    lg_ref[0] = jnp.dot(h, wr_ref[...], preferred_element_type=jnp.float32,
                        precision=lax.Precision.HIGHEST) + br_ref[...]


def _norm_mod(xa, g, mod_all, n_lat_tiles, shift_idx, scale_idx, router=None):
    B, NTOK, D = xa.shape
    tm = TOKEN_TILE
    grid = (B, NTOK // tm)
    x_spec = pl.BlockSpec((1, tm, D), lambda b, t: (b, t, 0))
    g_spec = pl.BlockSpec((1, D), lambda b, t: (0, 0))
    mod_spec = pl.BlockSpec((1, 1, 6, D), lambda b, t: (b, jnp.where(t >= n_lat_tiles, 1, 0), 0, 0))
    h_spec = pl.BlockSpec((1, tm, D), lambda b, t: (b, t, 0))
    params = _vmem_params(8 * tm * D * 4, ("parallel", "parallel"))
    if router is None:
        return pl.pallas_call(
            functools.partial(_norm_mod_kernel, shift_idx=shift_idx, scale_idx=scale_idx),
            grid=grid, in_specs=[x_spec, g_spec, mod_spec], out_specs=h_spec,
            out_shape=jax.ShapeDtypeStruct((B, NTOK, D), jnp.bfloat16),
            compiler_params=params, name="norm_mod")(xa, g.reshape(1, D), mod_all)
    w_router, b_router = router
    E = w_router.shape[1]
    return pl.pallas_call(
        functools.partial(_norm_mod_router_kernel, shift_idx=shift_idx, scale_idx=scale_idx),
        grid=grid,
        in_specs=[x_spec, g_spec, mod_spec, pl.BlockSpec((D, E), lambda b, t: (0, 0)),
                  pl.BlockSpec((1, E), lambda b, t: (0, 0))],
        out_specs=[h_spec, pl.BlockSpec((1, tm, E), lambda b, t: (b, t, 0))],
        out_shape=[jax.ShapeDtypeStruct((B, NTOK, D), jnp.bfloat16),
                   jax.ShapeDtypeStruct((B, NTOK, E), jnp.float32)],
        compiler_params=params, name="norm_mod_router")(xa, g.reshape(1, D), mod_all, w_router,
                                                         b_router.reshape(1, E))


def _mm_kernel(a_ref, w_ref, o_ref):
    o_ref[...] = jnp.dot(a_ref[...], w_ref[...], preferred_element_type=jnp.float32).astype(o_ref.dtype)


def _matmul(a, w, out_dtype, tm=512, tn=512):
    M, K = a.shape
    N = w.shape[1]
    assert M % tm == 0 and N % tn == 0, (M, N, tm, tn)
    nbytes = 2 * (tm * K * a.dtype.itemsize + K * tn * w.dtype.itemsize + tm * tn * 4)
    return pl.pallas_call(
        _mm_kernel, grid=(N // tn, M // tm),
        in_specs=[pl.BlockSpec((tm, K), lambda j, i: (i, 0)), pl.BlockSpec((K, tn), lambda j, i: (0, j))],
        out_specs=pl.BlockSpec((tm, tn), lambda j, i: (i, j)),
        out_shape=jax.ShapeDtypeStruct((M, N), out_dtype),
        compiler_params=_vmem_params(nbytes, ("parallel", "parallel")), name="matmul")(a, w)


def _merge_kernel(ya_ref, yb_ref, yc_ref, yd_ref, gl_ref, wb_ref, wo_ref, x_ref, mod_ref, o_ref, *, d_model):
    m = None
    for i, y_ref in enumerate((ya_ref, yb_ref, yc_ref, yd_ref)):
        gates = jax.nn.sigmoid(gl_ref[0, :, i * d_model:(i + 1) * d_model].astype(jnp.float32))
        p = gates * jnp.dot(y_ref[0], wb_ref[i], preferred_element_type=jnp.float32)
        m = p if m is None else m + p
    out = jnp.dot(m.astype(jnp.bfloat16), wo_ref[...], preferred_element_type=jnp.float32)
    o_ref[0] = x_ref[0] + mod_ref[0, 0, 2:3, :] * out


def _merge(ys, gate_logits, w_branch, w_out, xa, mod_all, n_lat_tiles):
    B, NTOK, D = xa.shape
    tm = TOKEN_TILE
    tok = lambda w: pl.BlockSpec((1, tm, w), lambda b, t: (b, t, 0))
    nbytes = (2 * (4 * tm * BRANCH_W * 2 + tm * 4 * D * 4 + 2 * tm * D * 4)
              + 2 * (N_BRANCH * BRANCH_W * D * 2 + D * D * 2) + 4 * tm * D * 4)
    return pl.pallas_call(
        functools.partial(_merge_kernel, d_model=D), grid=(B, NTOK // tm),
        in_specs=[tok(BRANCH_W)] * 4 + [
            tok(N_BRANCH * D),
            pl.BlockSpec((N_BRANCH, BRANCH_W, D), lambda b, t: (0, 0, 0)),
            pl.BlockSpec((D, D), lambda b, t: (0, 0)),
            tok(D),
            pl.BlockSpec((1, 1, 6, D), lambda b, t: (b, jnp.where(t >= n_lat_tiles, 1, 0), 0, 0))],
        out_specs=tok(D), out_shape=jax.ShapeDtypeStruct((B, NTOK, D), jnp.float32),
        compiler_params=_vmem_params(nbytes, ("parallel", "parallel")), name="merge")(
            *ys, gate_logits, w_branch, w_out, xa, mod_all)


def _moe_kernel(te_ref, nv_ref, x_ref, wgu_ref, bgu_ref, wdn_ref, bdn_ref, o_ref, *, d_expert):
    i = pl.program_id(0)

    @pl.when(i < nv_ref[0])
    def _():
        gu = jnp.dot(x_ref[...], wgu_ref[0], preferred_element_type=jnp.float32) + bgu_ref[0]
        gate = jnp.minimum(gu[:, :d_expert], SWIGLU_LIMIT)
        up = jnp.clip(gu[:, d_expert:], -SWIGLU_LIMIT, SWIGLU_LIMIT)
        act = (up + 1) * gate * jax.nn.sigmoid(SWIGLU_ALPHA * gate)
        y = jnp.dot(act.astype(jnp.bfloat16), wdn_ref[0], preferred_element_type=jnp.float32) + bdn_ref[0]
        o_ref[...] = y.astype(o_ref.dtype)

    @pl.when(i >= nv_ref[0])
    def _():
        o_ref[...] = jnp.zeros_like(o_ref)


def _moe_experts(x_sorted, tile_expert, n_valid, w_gu, b_gu, w_dn, b_dn):
    P, D = x_sorted.shape
    E, _, DE2 = w_gu.shape
    DE = DE2 // 2
    tm = MOE_TILE
    grid_spec = pltpu.PrefetchScalarGridSpec(
        num_scalar_prefetch=2, grid=(P // tm,),
        in_specs=[pl.BlockSpec((tm, D), lambda i, te, nv: (i, 0)),
                  pl.BlockSpec((1, D, DE2), lambda i, te, nv: (te[i], 0, 0)),
                  pl.BlockSpec((1, 1, DE2), lambda i, te, nv: (te[i], 0, 0)),
                  pl.BlockSpec((1, DE, D), lambda i, te, nv: (te[i], 0, 0)),
                  pl.BlockSpec((1, 1, D), lambda i, te, nv: (te[i], 0, 0))],
        out_specs=pl.BlockSpec((tm, D), lambda i, te, nv: (i, 0)))
    nbytes = 2 * (tm * D * 2 + D * DE2 * 2 + DE * D * 2 + tm * D * 4) + tm * DE2 * 4 * 3
    return pl.pallas_call(
        functools.partial(_moe_kernel, d_expert=DE), grid_spec=grid_spec,
        out_shape=jax.ShapeDtypeStruct((P, D), jnp.float32),
        compiler_params=_vmem_params(nbytes, ("arbitrary",)), name="moe_experts")(
            tile_expert, n_valid, x_sorted, w_gu, b_gu.reshape(E, 1, DE2), w_dn, b_dn.reshape(E, 1, D))


def _moe(h2, logits, w_gu, b_gu, w_dn, b_dn):
    T, D = h2.shape
    E = logits.shape[1]
    tm = MOE_TILE
    top_v, top_i = lax.top_k(logits, TOP_K)
    top_w = jax.nn.softmax(top_v, axis=-1)
    flat_e = top_i.reshape(-1).astype(jnp.int32)
    n_pairs = T * TOP_K
    counts = jnp.sum(jax.nn.one_hot(flat_e, E, dtype=jnp.int32), axis=0)
    padded = ((counts + tm - 1) // tm) * tm
    pad_end = jnp.cumsum(padded)
    pad_start = pad_end - padded
    start = jnp.cumsum(counts) - counts
    order = jnp.argsort(flat_e, stable=True)
    rank_sorted = jnp.arange(n_pairs, dtype=jnp.int32) - start[flat_e[order]]
    dest_sorted = pad_start[flat_e[order]] + rank_sorted
    P = n_pairs + E * tm
    slot_token = jnp.zeros((P,), jnp.int32).at[dest_sorted].set((order // TOP_K).astype(jnp.int32))
    dest = jnp.zeros((n_pairs,), jnp.int32).at[order].set(dest_sorted)
    n_valid = (pad_end[-1] // tm).astype(jnp.int32).reshape(1)
    tile_start = jnp.arange(P // tm, dtype=jnp.int32) * tm
    tile_expert = jnp.minimum(jnp.searchsorted(pad_end, tile_start, side='right'), E - 1).astype(jnp.int32)
    last_expert = tile_expert[jnp.maximum(n_valid[0] - 1, 0)]
    tile_expert = jnp.where(tile_start < pad_end[-1], tile_expert, last_expert)
    x_sorted = jnp.take(h2, slot_token, axis=0)
    y_sorted = _moe_experts(x_sorted, tile_expert, n_valid, w_gu, b_gu, w_dn, b_dn)
    y = jnp.take(y_sorted, dest, axis=0).reshape(T, TOP_K, D)
    return jnp.sum(y * top_w[..., None], axis=1)


def _final_norm_kernel(x_ref, g_ref, o_ref):
    x = x_ref[0]
    o_ref[0] = x * lax.rsqrt(jnp.mean(x * x, axis=-1, keepdims=True) + EPS) * g_ref[...]


def _final_norm(xa, g, seq):
    B, _, D = xa.shape
    tm = TOKEN_TILE
    return pl.pallas_call(
        _final_norm_kernel, grid=(B, seq // tm),
        in_specs=[pl.BlockSpec((1, tm, D), lambda b, t: (b, t, 0)), pl.BlockSpec((1, D), lambda b, t: (0, 0))],
        out_specs=pl.BlockSpec((1, tm, D), lambda b, t: (b, t, 0)),
        out_shape=jax.ShapeDtypeStruct((B, seq, D), jnp.float32),
        compiler_params=_vmem_params(4 * tm * D * 4, ("parallel", "parallel")), name="final_norm")(
            xa, g.reshape(1, D))


def _seg_mean_matrix(width, seg):
    idx = np.arange(width) // seg
    return jnp.asarray((idx[:, None] == idx[None, :]).astype(np.float32) / seg, jnp.bfloat16)


def _seg_mean_sq(x, m_ref):
    sq = x * x
    hi = sq.astype(jnp.bfloat16)
    lo = (sq - hi.astype(jnp.float32)).astype(jnp.bfloat16)
    return (jnp.dot(hi, m_ref[...], preferred_element_type=jnp.float32)
            + jnp.dot(lo, m_ref[...], preferred_element_type=jnp.float32))


def _rope(x, cos, sin_signed):
    width = x.shape[1]
    half = HEAD_DIM // 2
    lane = lax.broadcasted_iota(jnp.int32, x.shape, 1)
    fwd = pltpu.roll(x, width - half, axis=1)
    bwd = pltpu.roll(x, half, axis=1)
    rot = jnp.where((lane & (HEAD_DIM - 1)) < half, fwd, bwd)
    return x * cos + rot * sin_signed


def _rope_tables(seq, n_ctx):
    t = jnp.arange(seq)
    row = (t // GRID_W).astype(jnp.float32)
    col = (t % GRID_W).astype(jnp.float32)
    n_freq = HEAD_DIM // 4
    inv = ROPE_THETA ** (-jnp.arange(n_freq, dtype=jnp.float32) / n_freq)
    ang = jnp.concatenate([row[:, None] * inv[None], col[:, None] * inv[None]], axis=-1)
    cos, sin = jnp.cos(ang), jnp.sin(ang)
    reps = V7X_LANES // HEAD_DIM
    cos_t = jnp.tile(jnp.concatenate([cos, cos], axis=-1), (1, reps))
    sin_t = jnp.tile(jnp.concatenate([-sin, sin], axis=-1), (1, reps))
    cos_t = jnp.concatenate([cos_t, jnp.ones((n_ctx, V7X_LANES), jnp.float32)], axis=0)
    sin_t = jnp.concatenate([sin_t, jnp.zeros((n_ctx, V7X_LANES), jnp.float32)], axis=0)
    return cos_t, sin_t


_A_QW = A_HEADS * HEAD_DIM
_A_KW = A_KV_HEADS * HEAD_DIM


def _gqa_prep_kernel(u_ref, cos_ref, sin_ref, gq_ref, gk_ref, mq_ref, mk_ref, q_ref, k_ref):
    u = u_ref[0].astype(jnp.float32)
    q = u[:, :_A_QW]
    k = u[:, _A_QW:_A_QW + _A_KW]
    qn = q * lax.rsqrt(_seg_mean_sq(q, mq_ref) + EPS) * gq_ref[...]
    kn = k * lax.rsqrt(_seg_mean_sq(k, mk_ref) + EPS) * gk_ref[...]
    cos = cos_ref[...]
    sin = sin_ref[...]
    reps = _A_QW // V7X_LANES
    qr = _rope(qn, jnp.concatenate([cos] * reps, axis=1), jnp.concatenate([sin] * reps, axis=1))
    kr = _rope(kn, cos, sin)
    q_ref[0] = (qr * (HEAD_DIM ** -0.5)).astype(q_ref.dtype)
    k_ref[0] = kr.astype(k_ref.dtype)


def _gqa_prep(ua, cos_t, sin_t, g_q, g_k):
    B, NTOK, W = ua.shape
    tm = TOKEN_TILE
    gq = jnp.tile(g_q, A_HEADS).reshape(1, _A_QW)
    gk = jnp.tile(g_k, A_KV_HEADS).reshape(1, _A_KW)
    mq = _seg_mean_matrix(_A_QW, HEAD_DIM)
    mk = _seg_mean_matrix(_A_KW, HEAD_DIM)
    const = lambda shape: pl.BlockSpec(shape, lambda b, t: (0,) * len(shape))
    return pl.pallas_call(
        _gqa_prep_kernel, grid=(B, NTOK // tm),
        in_specs=[pl.BlockSpec((1, tm, W), lambda b, t: (b, t, 0)),
                  pl.BlockSpec((tm, V7X_LANES), lambda b, t: (t, 0)),
                  pl.BlockSpec((tm, V7X_LANES), lambda b, t: (t, 0)),
                  const((1, _A_QW)), const((1, _A_KW)), const((_A_QW, _A_QW)), const((_A_KW, _A_KW))],
        out_specs=[pl.BlockSpec((1, tm, _A_QW), lambda b, t: (b, t, 0)),
                   pl.BlockSpec((1, tm, _A_KW), lambda b, t: (b, t, 0))],
        out_shape=[jax.ShapeDtypeStruct((B, NTOK, _A_QW), jnp.bfloat16),
                   jax.ShapeDtypeStruct((B, NTOK, _A_KW), jnp.bfloat16)],
        compiler_params=_vmem_params(16 * tm * W * 4, ("parallel", "parallel")), name="gqa_prep")(
            ua, cos_t, sin_t, gq, gk, mq, mk)


def _dot_nt(a, b):
    return lax.dot_general(a, b, (((1,), (1,)), ((), ())), preferred_element_type=jnp.float32)


def _softmax_pv(scores, values):
    m = None
    for s in scores:
        mi = jnp.max(s, axis=-1, keepdims=True)
        m = mi if m is None else jnp.maximum(m, mi)
    acc, den = None, None
    for s, v in zip(scores, values):
        p = jnp.exp(s - m)
        li = jnp.sum(p, axis=-1, keepdims=True)
        oi = jnp.dot(p.astype(jnp.bfloat16), v, preferred_element_type=jnp.float32)
        acc = oi if acc is None else acc + oi
        den = li if den is None else den + li
    return acc / den


def _gqa_attn_kernel(q_ref, k_ref, v_ref, o_ref, *, seq, n_lat_tiles):
    t = pl.program_id(1)
    ntok = k_ref.shape[1]

    def attend(lo):
        outs = []
        for j in range(A_KV_HEADS):
            kv = slice(j * HEAD_DIM, (j + 1) * HEAD_DIM)
            kj = k_ref[0, lo:ntok, kv]
            vj = v_ref[0, lo:ntok, kv]
            for g in range(A_GROUP):
                h = j * A_GROUP + g
                qh = q_ref[0, :, h * HEAD_DIM:(h + 1) * HEAD_DIM]
                outs.append(_softmax_pv([_dot_nt(qh, kj)], [vj]))
        o_ref[0] = jnp.concatenate(outs, axis=-1).astype(o_ref.dtype)

    @pl.when(t < n_lat_tiles)
    def _():
        attend(0)

    @pl.when(t >= n_lat_tiles)
    def _():
        attend(seq)


def _gqa_attn(qn, kn, ua, seq):
    B, NTOK, _ = qn.shape
    tq = TOKEN_TILE
    v_block = (_A_QW + _A_KW) // _A_KW
    nbytes = 4 * NTOK * _A_KW * 2 + 4 * tq * _A_QW * 2 + 6 * tq * NTOK * 4
    return pl.pallas_call(
        functools.partial(_gqa_attn_kernel, seq=seq, n_lat_tiles=seq // tq), grid=(B, NTOK // tq),
        in_specs=[pl.BlockSpec((1, tq, _A_QW), lambda b, t: (b, t, 0)),
                  pl.BlockSpec((1, NTOK, _A_KW), lambda b, t: (b, 0, 0)),
                  pl.BlockSpec((1, NTOK, _A_KW), lambda b, t: (b, 0, v_block))],
        out_specs=pl.BlockSpec((1, tq, _A_QW), lambda b, t: (b, t, 0)),
        out_shape=jax.ShapeDtypeStruct((B, NTOK, _A_QW), jnp.bfloat16),
        compiler_params=_vmem_params(nbytes, ("parallel", "arbitrary")), name="gqa_attn")(qn, kn, ua)


_NA_W = NA_HEADS * HEAD_DIM
_NA_KBLOCK = 3 * TOKEN_TILE
_NA_MASKED = -1e30


def _na_bias_table(rpb, seq):
    rows = seq // GRID_W
    rows_per_tile = TOKEN_TILE // GRID_W
    n_tiles = rows // rows_per_tile
    assert TOKEN_TILE % GRID_W == 0 and rows % rows_per_tile == 0 and n_tiles >= 3 and rows >= NA_WIN_R
    assert rows_per_tile + NA_WIN_R <= _NA_KBLOCK // GRID_W
    tables = []
    for t in (0, 1, n_tiles - 1):
        i = np.arange(TOKEN_TILE)
        qr, qc = t * rows_per_tile + i // GRID_W, i % GRID_W
        j = np.arange(_NA_KBLOCK)
        kr0 = int(np.clip(t - 1, 0, n_tiles - 3)) * rows_per_tile
        kr, kc = kr0 + j // GRID_W, j % GRID_W
        rs = np.clip(qr - NA_WIN_R // 2, 0, rows - NA_WIN_R)
        cs = np.clip(qc - NA_WIN_C // 2, 0, GRID_W - NA_WIN_C)
        valid = ((kr[None] >= rs[:, None]) & (kr[None] < rs[:, None] + NA_WIN_R)
                 & (kc[None] >= cs[:, None]) & (kc[None] < cs[:, None] + NA_WIN_C))
        ridx = np.clip(kr[None] - qr[:, None] + NA_WIN_R - 1, 0, 2 * NA_WIN_R - 2)
        cidx = np.clip(kc[None] - qc[:, None] + NA_WIN_C - 1, 0, 2 * NA_WIN_C - 2)
        tables.append(jnp.where(valid[None], rpb[:, ridx, cidx], _NA_MASKED))
    return jnp.stack(tables)


def _na_attn_kernel(q_ref, k_ref, v_ref, bias_ref, o_ref, *, seq, n_lat_tiles):
    t = pl.program_id(1)
    ntok = k_ref.shape[1]
    scale = HEAD_DIM ** -0.5

    @pl.when(t < n_lat_tiles)
    def _():
        k_off = pl.multiple_of(jnp.clip(t - 1, 0, n_lat_tiles - 3) * TOKEN_TILE, TOKEN_TILE)
        outs = []
        for h in range(NA_HEADS):
            hs = slice(h * HEAD_DIM, (h + 1) * HEAD_DIM)
            qh = q_ref[0, :, hs] * scale
            s_nb = _dot_nt(qh, k_ref[0, pl.ds(k_off, _NA_KBLOCK), hs]) + bias_ref[0, h]
            s_ctx = _dot_nt(qh, k_ref[0, seq:ntok, hs])
            outs.append(_softmax_pv([s_nb, s_ctx], [v_ref[0, pl.ds(k_off, _NA_KBLOCK), hs], v_ref[0, seq:ntok, hs]]))
        o_ref[0] = jnp.concatenate(outs, axis=-1).astype(o_ref.dtype)

    @pl.when(t >= n_lat_tiles)
    def _():
        outs = []
        for h in range(NA_HEADS):
            hs = slice(h * HEAD_DIM, (h + 1) * HEAD_DIM)
            qh = q_ref[0, :, hs] * scale
            outs.append(_softmax_pv([_dot_nt(qh, k_ref[0, seq:ntok, hs])], [v_ref[0, seq:ntok, hs]]))
        o_ref[0] = jnp.concatenate(outs, axis=-1).astype(o_ref.dtype)


def _na_attn(ud, bias, seq):
    B, NTOK, _ = ud.shape
    tq = TOKEN_TILE
    n_lat_tiles = seq // tq
    nbytes = (4 * NTOK * _NA_W * 2 + 4 * tq * _NA_W * 2 + 2 * NA_HEADS * tq * _NA_KBLOCK * 4
              + 8 * tq * (_NA_KBLOCK + NTOK - seq) * 4)
    return pl.pallas_call(
        functools.partial(_na_attn_kernel, seq=seq, n_lat_tiles=n_lat_tiles), grid=(B, NTOK // tq),
        in_specs=[pl.BlockSpec((1, tq, _NA_W), lambda b, t: (b, t, 0)),
                  pl.BlockSpec((1, NTOK, _NA_W), lambda b, t: (b, 0, 1)),
                  pl.BlockSpec((1, NTOK, _NA_W), lambda b, t: (b, 0, 2)),
                  pl.BlockSpec((1, NA_HEADS, tq, _NA_KBLOCK),
                               lambda b, t: (jnp.where(t == 0, 0, jnp.where(t >= n_lat_tiles - 1, 2, 1)), 0, 0, 0))],
        out_specs=pl.BlockSpec((1, tq, _NA_W), lambda b, t: (b, t, 0)),
        out_shape=jax.ShapeDtypeStruct((B, NTOK, _NA_W), jnp.bfloat16),
        compiler_params=_vmem_params(nbytes, ("parallel", "arbitrary")), name="na_attn")(ud, ud, ud, bias)


def _rms_norm(x, g):
    xf = x.astype(jnp.float32)
    y = xf * lax.rsqrt(jnp.mean(xf * xf, axis=-1, keepdims=True) + EPS)
    return (y * g.astype(jnp.float32)).astype(x.dtype)


def _l2_normalize(x):
    xf = x.astype(jnp.float32)
    return xf * lax.rsqrt(jnp.sum(xf * xf, axis=-1, keepdims=True) + EPS)


def _flip_seq(ts):
    return tuple(jnp.flip(t, axis=1) for t in ts)


def _dwconv(x, w, b=None):
    y = lax.conv_general_dilated(x, w[:, None, :].astype(x.dtype), window_strides=(1,),
                                 padding=[CONV_PAD], dimension_numbers=('NWC', 'WIO', 'NWC'),
                                 feature_group_count=x.shape[-1])
    return y if b is None else y + b.astype(x.dtype)


def _rglru_coeffs(x, lam, w_r, b_r, w_i, b_i):
    xf = x.astype(jnp.float32)
    blk = xf.reshape(x.shape[:-1] + (LRU_BLOCKS, LRU_BLOCK_W))
    r = jax.nn.sigmoid(jnp.einsum('blnc,ncd->blnd', blk, w_r).reshape(x.shape) + b_r)
    i = jax.nn.sigmoid(jnp.einsum('blnc,ncd->blnd', blk, w_i).reshape(x.shape) + b_i)
    log_a = -LRU_C * r * jax.nn.softplus(-lam.astype(jnp.float32))
    return jnp.exp(log_a), jnp.sqrt(-jnp.expm1(2 * log_a)) * (i * xf)


def _lin_combine(e1, e2):
    a1, b1 = e1
    a2, b2 = e2
    return a1 * a2, a2 * b1 + b2


def _linear_scan(a, b, h0):
    b = b.at[:, 0].add(a[:, 0] * h0)
    _, h = lax.associative_scan(_lin_combine, (a, b), axis=1)
    return h


def _mixer_rglru(xb, yb, xbc, ybc, conv_w, conv_b, lam, w_r, b_r, w_i, b_i):
    xb = _dwconv(xb, conv_w, conv_b)
    xbc = _dwconv(xbc, conv_w, conv_b)
    h0 = jnp.zeros((xb.shape[0], LRU_WIDTH), jnp.float32)
    hs, hcs = [], []
    for d in range(2):
        a, bb = _rglru_coeffs(xb, lam[d], w_r[d], b_r[d], w_i[d], b_i[d])
        ac, bc = _rglru_coeffs(xbc, lam[d], w_r[d], b_r[d], w_i[d], b_i[d])
        if d == 1:
            a, bb, ac, bc = _flip_seq((a, bb, ac, bc))
        hc = _linear_scan(ac, bc, h0)
        h = _linear_scan(a, bb, hc[:, -1])
        if d == 1:
            h, hc = _flip_seq((h, hc))
        hs.append(h)
        hcs.append(hc)
    y = ((hs[0] + hs[1]) * jax.nn.gelu(yb.astype(jnp.float32))).astype(yb.dtype)
    yc = ((hcs[0] + hcs[1]) * jax.nn.gelu(ybc.astype(jnp.float32))).astype(ybc.dtype)
    return y, yc


def _chunk_gated_delta(q, k, v, g, beta, s0):
    B, L, H, _ = q.shape
    C = DN_CHUNK
    n = L // C

    def blk(t):
        return jnp.moveaxis(t.reshape((B, n, C, H) + t.shape[3:]), 3, 1)

    q, k, v, beta = blk(q), blk(k), blk(v), blk(beta)
    gc = jnp.cumsum(blk(g), axis=-1)
    kb = k * beta[..., None]
    vb = v * beta[..., None]
    lower = np.tril(np.ones((C, C), dtype=bool))
    diff = gc[..., :, None] - gc[..., None, :]
    decay = jnp.where(lower, jnp.exp(jnp.where(lower, diff, 0.0)), 0.0)
    lmat = jnp.einsum('bhncd,bhned->bhnce', kb, k) * decay
    u = lax.linalg.triangular_solve(lmat, vb, left_side=True, lower=True, unit_diagonal=True)
    w = lax.linalg.triangular_solve(lmat, kb * jnp.exp(gc)[..., None], left_side=True, lower=True,
                                    unit_diagonal=True)
    qk = jnp.einsum('bhncd,bhned->bhnce', q, k) * decay
    qg = q * jnp.exp(gc)[..., None]
    kd = k * jnp.exp(gc[..., -1:] - gc)[..., None]
    glast = jnp.exp(gc[..., -1])

    def step(s, xs):
        qk_i, qg_i, w_i, u_i, kd_i, gl_i = xs
        vn = u_i - jnp.einsum('bhcd,bhde->bhce', w_i, s)
        o = jnp.einsum('bhcd,bhde->bhce', qg_i, s) + jnp.einsum('bhce,bhef->bhcf', qk_i, vn)
        s = s * gl_i[..., None, None] + jnp.einsum('bhcd,bhce->bhde', kd_i, vn)
        return s, o

    xs = tuple(jnp.moveaxis(t, 2, 0) for t in (qk, qg, w, u, kd, glast))
    s_fin, o = lax.scan(step, s0, xs)
    o = jnp.moveaxis(jnp.moveaxis(o, 0, 2), 1, 3).reshape(B, L, H, DN_DV)
    return o, s_fin


def _gdn_prep(q, k, v, ga, gb, conv_w, a_log, dt_bias):
    B, L, _ = q.shape
    nk = DN_HEADS * DN_DK
    qkv = jax.nn.silu(_dwconv(jnp.concatenate([q, k, v], axis=-1), conv_w).astype(jnp.float32))
    qh = _l2_normalize(qkv[..., :nk].reshape(B, L, DN_HEADS, DN_DK)) * (DN_DK ** -0.5)
    kh = _l2_normalize(qkv[..., nk:2 * nk].reshape(B, L, DN_HEADS, DN_DK))
    vh = qkv[..., 2 * nk:].reshape(B, L, DN_HEADS, DN_DV)
    g = -jnp.exp(a_log.astype(jnp.float32)) * jax.nn.softplus(
        ga.astype(jnp.float32).reshape(B, L, 2, DN_HEADS) + dt_bias.astype(jnp.float32))
    beta = jax.nn.sigmoid(gb.astype(jnp.float32).reshape(B, L, 2, DN_HEADS))
    return qh, kh, vh, g, beta


def _gdn_out(o, z, g_out):
    B, L, _ = z.shape
    zf = jax.nn.silu(z.astype(jnp.float32).reshape(B, L, DN_HEADS, DN_DV))
    return (_rms_norm(o, g_out) * zf).reshape(B, L, DN_HEADS * DN_DV).astype(z.dtype)


def _mixer_gdn(q, k, v, z, ga, gb, qc, kc, vc, zc, gac, gbc, conv_w, a_log, dt_bias, g_out):
    qh, kh, vh, g, beta = _gdn_prep(q, k, v, ga, gb, conv_w, a_log, dt_bias)
    qch, kch, vch, gcx, bcx = _gdn_prep(qc, kc, vc, gac, gbc, conv_w, a_log, dt_bias)
    s_zero = jnp.zeros((q.shape[0], DN_HEADS, DN_DK, DN_DV), jnp.float32)
    outs, outs_c = [], []
    for d in range(2):
        lat_in = (qh, kh, vh, g[:, :, d], beta[:, :, d])
        ctx_in = (qch, kch, vch, gcx[:, :, d], bcx[:, :, d])
        if d == 1:
            lat_in, ctx_in = _flip_seq(lat_in), _flip_seq(ctx_in)
        oc, s_ctx = _chunk_gated_delta(*ctx_in, s_zero)
        o, _ = _chunk_gated_delta(*lat_in, s_ctx)
        if d == 1:
            o, oc = _flip_seq((o, oc))
        outs.append(o)
        outs_c.append(oc)
    y = _gdn_out(outs[0] + outs[1], z, g_out)
    yc = _gdn_out(outs_c[0] + outs_c[1], zc, g_out)
    return y, yc


def kernel(x, c, ctx, c_ctx, w_mod, b_mod, g_norm1, g_norm2, w_in, a_gq, a_gk, b_conv_w, b_conv_b, b_lam,
           b_wr, b_br, b_wi, b_bi, c_conv_w, c_alog, c_dtb, c_gout, d_rpb, w_branch, w_out, w_router,
           b_router, w_gu, b_gu, w_dn, b_dn, g_final):
    B, S, D = x.shape
    L = ctx.shape[1]
    depth = w_mod.shape[0]
    NTOK = S + L
    assert S % TOKEN_TILE == 0 and L % TOKEN_TILE == 0
    n_lat_tiles = S // TOKEN_TILE
    col = {}
    off = 0
    for name, width in _in_splits(D):
        col[name] = (off, off + width)
        off += width
    bf16 = jnp.bfloat16

    def w_cols(w, names, pad_to=None):
        parts = [w[:, col[n][0]:col[n][1]] for n in names]
        out = parts[0] if len(parts) == 1 else jnp.concatenate(parts, axis=1)
        if pad_to is not None:
            out = jnp.pad(out, ((0, 0), (0, pad_to - out.shape[1])))
        return out.astype(bf16)

    cos_t, sin_t = _rope_tables(S, L)
    sc = jax.nn.silu(c)
    scc = jax.nn.silu(c_ctx)
    xa = jnp.concatenate([x, ctx], axis=1)

    for l in range(depth):
        mod = sc @ w_mod[l] + b_mod[l]
        mod_c = scc @ w_mod[l] + b_mod[l]
        mod_all = jnp.stack([mod.reshape(B, 6, D), jnp.broadcast_to(mod_c.reshape(1, 6, D), (B, 6, D))], axis=1)

        h = _norm_mod(xa, g_norm1[l], mod_all, n_lat_tiles, 0, 1)
        hf = h.reshape(B * NTOK, D)
        proj = lambda names, dtype, tn, pad_to=None: _matmul(
            hf, w_cols(w_in[l], names, pad_to), dtype, tn=tn).reshape(B, NTOK, -1)
        ua = proj(('a_q', 'a_k', 'a_v'), bf16, 768)
        ub = proj(('b_x', 'b_y'), jnp.float32, 512)
        uc_ = proj(('c_q', 'c_k', 'c_v', 'c_z'), jnp.float32, 512)
        uab = proj(('c_a', 'c_b'), jnp.float32, V7X_LANES, pad_to=V7X_LANES)
        ud = proj(('d_q', 'd_k', 'd_v'), bf16, 768)
        gate_logits = proj(('gate',), bf16, min(1024, N_BRANCH * D))

        qn, kn = _gqa_prep(ua, cos_t, sin_t, a_gq[l], a_gk[l])
        ya = _gqa_attn(qn, kn, ua, S)
        yd = _na_attn(ud, _na_bias_table(d_rpb[l], S), S)

        lw, dw = LRU_WIDTH, DN_HEADS * DN_DK
        nh = DN_HEADS
        yb, ybc = _mixer_rglru(ub[:, :S, :lw], ub[:, :S, lw:], ub[:, S:, :lw], ub[:, S:, lw:], b_conv_w[l],
                               b_conv_b[l], b_lam[l], b_wr[l], b_br[l], b_wi[l], b_bi[l])
        cq, ck, cv, cz = (uc_[..., i * dw:(i + 1) * dw] for i in range(4))
        ca, cb = uab[..., :2 * nh], uab[..., 2 * nh:4 * nh]
        yc, ycc = _mixer_gdn(cq[:, :S], ck[:, :S], cv[:, :S], cz[:, :S], ca[:, :S], cb[:, :S],
                             cq[:, S:], ck[:, S:], cv[:, S:], cz[:, S:], ca[:, S:], cb[:, S:],
                             c_conv_w[l], c_alog[l], c_dtb[l], c_gout[l])
        ys = [ya, jnp.concatenate([yb, ybc], axis=1).astype(bf16),
              jnp.concatenate([yc, ycc], axis=1).astype(bf16), yd]

        xa = _merge(ys, gate_logits, w_branch[l].astype(bf16), w_out[l].astype(bf16), xa, mod_all, n_lat_tiles)
        h2, logits = _norm_mod(xa, g_norm2[l], mod_all, n_lat_tiles, 3, 4, router=(w_router[l], b_router[l]))
        f = _moe(h2.reshape(B * NTOK, D), logits.reshape(B * NTOK, -1),
                 w_gu[l].astype(bf16), b_gu[l], w_dn[l].astype(bf16), b_dn[l])
        gt2 = mod_all[:, :, 5]
        gt2_rows = jnp.concatenate([jnp.broadcast_to(gt2[:, 0:1], (B, S, D)),
                                    jnp.broadcast_to(gt2[:, 1:2], (B, L, D))], axis=1)
        xa = xa + gt2_rows * f.reshape(B, NTOK, D)

    return _final_norm(xa, g_final, S)
```

```python
import functools

import jax
import jax.numpy as jnp
import numpy as np
from jax import lax
from jax.experimental import pallas as pl
from jax.experimental.pallas import tpu as pltpu

GRID_W = 64
HEAD_DIM = 64
EPS = 1e-6

A_HEADS = 8
A_KV_HEADS = 2
A_GROUP = A_HEADS // A_KV_HEADS
A_BLOCK = 128
ROPE_THETA = 10000.0

LRU_WIDTH = 512
LRU_BLOCKS = 8
LRU_BLOCK_W = LRU_WIDTH // LRU_BLOCKS
LRU_C = 8.0

CONV_W = 4
CONV_PAD = ((CONV_W - 1) // 2, CONV_W // 2)

DN_HEADS = 4
DN_DK = 128
DN_DV = 128
DN_CHUNK = 64

NA_HEADS = 8
NA_WIN_R = 8
NA_WIN_C = 16

N_BRANCH = 4
BRANCH_W = 512

TOP_K = 4
SWIGLU_LIMIT = 7.0
SWIGLU_ALPHA = 1.702

V7X_LANES = 128
V7X_MXU_DIM = 256
V7X_VMEM_BYTES = 64 * 1024 * 1024

TOKEN_TILE = 256
MOE_TILE = 256


def _in_splits(d_model):
    return (
        ('a_q', A_HEADS * HEAD_DIM), ('a_k', A_KV_HEADS * HEAD_DIM), ('a_v', A_KV_HEADS * HEAD_DIM),
        ('b_x', LRU_WIDTH), ('b_y', LRU_WIDTH),
        ('c_q', DN_HEADS * DN_DK), ('c_k', DN_HEADS * DN_DK), ('c_v', DN_HEADS * DN_DV),
        ('c_z', DN_HEADS * DN_DV), ('c_a', 2 * DN_HEADS), ('c_b', 2 * DN_HEADS),
        ('d_q', NA_HEADS * HEAD_DIM), ('d_k', NA_HEADS * HEAD_DIM), ('d_v', NA_HEADS * HEAD_DIM),
        ('gate', N_BRANCH * d_model),
    )


def _vmem_params(nbytes, semantics):
    limit = int(min(max(2 * nbytes, 32 * 1024 * 1024), V7X_VMEM_BYTES - 8 * 1024 * 1024))
    return pltpu.CompilerParams(dimension_semantics=semantics, vmem_limit_bytes=limit)


def _norm_mod_kernel(x_ref, g_ref, mod_ref, h_ref, *, shift_idx, scale_idx):
    x = x_ref[0]
    y = x * lax.rsqrt(jnp.mean(x * x, axis=-1, keepdims=True) + EPS) * g_ref[...]
    shift = mod_ref[0, 0, shift_idx:shift_idx + 1, :]
    scale = mod_ref[0, 0, scale_idx:scale_idx + 1, :]
    h_ref[0] = (y * (1 + scale) + shift).astype(h_ref.dtype)


def _norm_mod_router_kernel(x_ref, g_ref, mod_ref, wr_ref, br_ref, h_ref, lg_ref, *, shift_idx, scale_idx):
    x = x_ref[0]
    y = x * lax.rsqrt(jnp.mean(x * x, axis=-1, keepdims=True) + EPS) * g_ref[...]
    shift = mod_ref[0, 0, shift_idx:shift_idx + 1, :]
    scale = mod_ref[0, 0, scale_idx:scale_idx + 1, :]
    h = y * (1 + scale) + shift
    h_ref[0] = h.astype(h_ref.dtype)
    lg_ref[0] = jnp.dot(h, wr_ref[...], preferred_element_type=jnp.float32,
                        precision=lax.Precision.HIGHEST) + br_ref[...]


def _norm_mod(xa, g, mod_all, n_lat_tiles, shift_idx, scale_idx, router=None):
    B, NTOK, D = xa.shape
    tm = TOKEN_TILE
    grid = (B, NTOK // tm)
    x_spec = pl.BlockSpec((1, tm, D), lambda b, t: (b, t, 0))
    g_spec = pl.BlockSpec((1, D), lambda b, t: (0, 0))
    mod_spec = pl.BlockSpec((1, 1, 6, D), lambda b, t: (b, jnp.where(t >= n_lat_tiles, 1, 0), 0, 0))
    h_spec = pl.BlockSpec((1, tm, D), lambda b, t: (b, t, 0))
    params = _vmem_params(8 * tm * D * 4, ("parallel", "parallel"))
    if router is None:
        return pl.pallas_call(
            functools.partial(_norm_mod_kernel, shift_idx=shift_idx, scale_idx=scale_idx),
            grid=grid, in_specs=[x_spec, g_spec, mod_spec], out_specs=h_spec,
            out_shape=jax.ShapeDtypeStruct((B, NTOK, D), jnp.bfloat16),
            compiler_params=params, name="norm_mod")(xa, g.reshape(1, D), mod_all)
    w_router, b_router = router
    E = w_router.shape[1]
    return pl.pallas_call(
        functools.partial(_norm_mod_router_kernel, shift_idx=shift_idx, scale_idx=scale_idx),
        grid=grid,
        in_specs=[x_spec, g_spec, mod_spec, pl.BlockSpec((D, E), lambda b, t: (0, 0)),
                  pl.BlockSpec((1, E), lambda b, t: (0, 0))],
        out_specs=[h_spec, pl.BlockSpec((1, tm, E), lambda b, t: (b, t, 0))],
        out_shape=[jax.ShapeDtypeStruct((B, NTOK, D), jnp.bfloat16),
                   jax.ShapeDtypeStruct((B, NTOK, E), jnp.float32)],
        compiler_params=params, name="norm_mod_router")(xa, g.reshape(1, D), mod_all, w_router,
                                                         b_router.reshape(1, E))


def _mm_kernel(a_ref, w_ref, o_ref):
    o_ref[...] = jnp.dot(a_ref[...], w_ref[...], preferred_element_type=jnp.float32).astype(o_ref.dtype)


def _matmul(a, w, out_dtype, tm=512, tn=512):
    M, K = a.shape
    N = w.shape[1]
    assert M % tm == 0 and N % tn == 0, (M, N, tm, tn)
    nbytes = 2 * (tm * K * a.dtype.itemsize + K * tn * w.dtype.itemsize + tm * tn * 4)
    return pl.pallas_call(
        _mm_kernel, grid=(N // tn, M // tm),
        in_specs=[pl.BlockSpec((tm, K), lambda j, i: (i, 0)), pl.BlockSpec((K, tn), lambda j, i: (0, j))],
        out_specs=pl.BlockSpec((tm, tn), lambda j, i: (i, j)),
        out_shape=jax.ShapeDtypeStruct((M, N), out_dtype),
        compiler_params=_vmem_params(nbytes, ("parallel", "parallel")), name="matmul")(a, w)


def _merge_kernel(ya_ref, yb_ref, yc_ref, yd_ref, gl_ref, wb_ref, wo_ref, x_ref, mod_ref, o_ref, *, d_model):
    m = None
    for i, y_ref in enumerate((ya_ref, yb_ref, yc_ref, yd_ref)):
        gates = jax.nn.sigmoid(gl_ref[0, :, i * d_model:(i + 1) * d_model].astype(jnp.float32))
        p = gates * jnp.dot(y_ref[0], wb_ref[i], preferred_element_type=jnp.float32)
        m = p if m is None else m + p
    out = jnp.dot(m.astype(jnp.bfloat16), wo_ref[...], preferred_element_type=jnp.float32)
    o_ref[0] = x_ref[0] + mod_ref[0, 0, 2:3, :] * out


def _merge(ys, gate_logits, w_branch, w_out, xa, mod_all, n_lat_tiles):
    B, NTOK, D = xa.shape
    tm = TOKEN_TILE
    tok = lambda w: pl.BlockSpec((1, tm, w), lambda b, t: (b, t, 0))
    nbytes = (2 * (4 * tm * BRANCH_W * 2 + tm * 4 * D * 4 + 2 * tm * D * 4)
              + 2 * (N_BRANCH * BRANCH_W * D * 2 + D * D * 2) + 4 * tm * D * 4)
    return pl.pallas_call(
        functools.partial(_merge_kernel, d_model=D), grid=(B, NTOK // tm),
        in_specs=[tok(BRANCH_W)] * 4 + [
            tok(N_BRANCH * D),
            pl.BlockSpec((N_BRANCH, BRANCH_W, D), lambda b, t: (0, 0, 0)),
            pl.BlockSpec((D, D), lambda b, t: (0, 0)),
            tok(D),
            pl.BlockSpec((1, 1, 6, D), lambda b, t: (b, jnp.where(t >= n_lat_tiles, 1, 0), 0, 0))],
        out_specs=tok(D), out_shape=jax.ShapeDtypeStruct((B, NTOK, D), jnp.float32),
        compiler_params=_vmem_params(nbytes, ("parallel", "parallel")), name="merge")(
            *ys, gate_logits, w_branch, w_out, xa, mod_all)


def _moe_kernel(te_ref, nv_ref, x_ref, wgu_ref, bgu_ref, wdn_ref, bdn_ref, o_ref, *, d_expert):
    i = pl.program_id(0)

    @pl.when(i < nv_ref[0])
    def _():
        gu = jnp.dot(x_ref[...], wgu_ref[0], preferred_element_type=jnp.float32) + bgu_ref[0]
        gate = jnp.minimum(gu[:, :d_expert], SWIGLU_LIMIT)
        up = jnp.clip(gu[:, d_expert:], -SWIGLU_LIMIT, SWIGLU_LIMIT)
        act = (up + 1) * gate * jax.nn.sigmoid(SWIGLU_ALPHA * gate)
        y = jnp.dot(act.astype(jnp.bfloat16), wdn_ref[0], preferred_element_type=jnp.float32) + bdn_ref[0]
        o_ref[...] = y.astype(o_ref.dtype)

    @pl.when(i >= nv_ref[0])
    def _():
        o_ref[...] = jnp.zeros_like(o_ref)


def _moe_experts(x_sorted, tile_expert, n_valid, w_gu, b_gu, w_dn, b_dn):
    P, D = x_sorted.shape
    E, _, DE2 = w_gu.shape
    DE = DE2 // 2
    tm = MOE_TILE
    grid_spec = pltpu.PrefetchScalarGridSpec(
        num_scalar_prefetch=2, grid=(P // tm,),
        in_specs=[pl.BlockSpec((tm, D), lambda i, te, nv: (i, 0)),
                  pl.BlockSpec((1, D, DE2), lambda i, te, nv: (te[i], 0, 0)),
                  pl.BlockSpec((1, 1, DE2), lambda i, te, nv: (te[i], 0, 0)),
                  pl.BlockSpec((1, DE, D), lambda i, te, nv: (te[i], 0, 0)),
                  pl.BlockSpec((1, 1, D), lambda i, te, nv: (te[i], 0, 0))],
        out_specs=pl.BlockSpec((tm, D), lambda i, te, nv: (i, 0)))
    nbytes = 2 * (tm * D * 2 + D * DE2 * 2 + DE * D * 2 + tm * D * 4) + tm * DE2 * 4 * 3
    return pl.pallas_call(
        functools.partial(_moe_kernel, d_expert=DE), grid_spec=grid_spec,
        out_shape=jax.ShapeDtypeStruct((P, D), jnp.float32),
        compiler_params=_vmem_params(nbytes, ("arbitrary",)), name="moe_experts")(
            tile_expert, n_valid, x_sorted, w_gu, b_gu.reshape(E, 1, DE2), w_dn, b_dn.reshape(E, 1, D))


def _moe(h2, logits, w_gu, b_gu, w_dn, b_dn):
    T, D = h2.shape
    E = logits.shape[1]
    tm = MOE_TILE
    top_v, top_i = lax.top_k(logits, TOP_K)
    top_w = jax.nn.softmax(top_v, axis=-1)
    flat_e = top_i.reshape(-1).astype(jnp.int32)
    n_pairs = T * TOP_K
    counts = jnp.sum(jax.nn.one_hot(flat_e, E, dtype=jnp.int32), axis=0)
    padded = ((counts + tm - 1) // tm) * tm
    pad_end = jnp.cumsum(padded)
    pad_start = pad_end - padded
    start = jnp.cumsum(counts) - counts
    order = jnp.argsort(flat_e, stable=True)
    rank_sorted = jnp.arange(n_pairs, dtype=jnp.int32) - start[flat_e[order]]
    dest_sorted = pad_start[flat_e[order]] + rank_sorted
    P = n_pairs + E * tm
    slot_token = jnp.zeros((P,), jnp.int32).at[dest_sorted].set((order // TOP_K).astype(jnp.int32))
    dest = jnp.zeros((n_pairs,), jnp.int32).at[order].set(dest_sorted)
    n_valid = (pad_end[-1] // tm).astype(jnp.int32).reshape(1)
    tile_start = jnp.arange(P // tm, dtype=jnp.int32) * tm
    tile_expert = jnp.minimum(jnp.searchsorted(pad_end, tile_start, side='right'), E - 1).astype(jnp.int32)
    last_expert = tile_expert[jnp.maximum(n_valid[0] - 1, 0)]
    tile_expert = jnp.where(tile_start < pad_end[-1], tile_expert, last_expert)
    x_sorted = jnp.take(h2, slot_token, axis=0)
    y_sorted = _moe_experts(x_sorted, tile_expert, n_valid, w_gu, b_gu, w_dn, b_dn)
    y = jnp.take(y_sorted, dest, axis=0).reshape(T, TOP_K, D)
    return jnp.sum(y * top_w[..., None], axis=1)


def _final_norm_kernel(x_ref, g_ref, o_ref):
    x = x_ref[0]
    o_ref[0] = x * lax.rsqrt(jnp.mean(x * x, axis=-1, keepdims=True) + EPS) * g_ref[...]


def _final_norm(xa, g, seq):
    B, _, D = xa.shape
    tm = TOKEN_TILE
    return pl.pallas_call(
        _final_norm_kernel, grid=(B, seq // tm),
        in_specs=[pl.BlockSpec((1, tm, D), lambda b, t: (b, t, 0)), pl.BlockSpec((1, D), lambda b, t: (0, 0))],
        out_specs=pl.BlockSpec((1, tm, D), lambda b, t: (b, t, 0)),
        out_shape=jax.ShapeDtypeStruct((B, seq, D), jnp.float32),
        compiler_params=_vmem_params(4 * tm * D * 4, ("parallel", "parallel")), name="final_norm")(
            xa, g.reshape(1, D))


def _seg_mean_matrix(width, seg):
    idx = np.arange(width) // seg
    return jnp.asarray((idx[:, None] == idx[None, :]).astype(np.float32) / seg, jnp.bfloat16)


def _seg_mean_sq(x, m_ref):
    sq = x * x
    hi = sq.astype(jnp.bfloat16)
    lo = (sq - hi.astype(jnp.float32)).astype(jnp.bfloat16)
    return (jnp.dot(hi, m_ref[...], preferred_element_type=jnp.float32)
            + jnp.dot(lo, m_ref[...], preferred_element_type=jnp.float32))


def _rope(x, cos, sin_signed):
    width = x.shape[1]
    half = HEAD_DIM // 2
    lane = lax.broadcasted_iota(jnp.int32, x.shape, 1)
    fwd = pltpu.roll(x, width - half, axis=1)
    bwd = pltpu.roll(x, half, axis=1)
    rot = jnp.where((lane & (HEAD_DIM - 1)) < half, fwd, bwd)
    return x * cos + rot * sin_signed


def _rope_tables(seq, n_ctx):
    t = jnp.arange(seq)
    row = (t // GRID_W).astype(jnp.float32)
    col = (t % GRID_W).astype(jnp.float32)
    n_freq = HEAD_DIM // 4
    inv = ROPE_THETA ** (-jnp.arange(n_freq, dtype=jnp.float32) / n_freq)
    ang = jnp.concatenate([row[:, None] * inv[None], col[:, None] * inv[None]], axis=-1)
    cos, sin = jnp.cos(ang), jnp.sin(ang)
    reps = V7X_LANES // HEAD_DIM
    cos_t = jnp.tile(jnp.concatenate([cos, cos], axis=-1), (1, reps))
    sin_t = jnp.tile(jnp.concatenate([-sin, sin], axis=-1), (1, reps))
    cos_t = jnp.concatenate([cos_t, jnp.ones((n_ctx, V7X_LANES), jnp.float32)], axis=0)
    sin_t = jnp.concatenate([sin_t, jnp.zeros((n_ctx, V7X_LANES), jnp.float32)], axis=0)
    return cos_t, sin_t


_A_QW = A_HEADS * HEAD_DIM
_A_KW = A_KV_HEADS * HEAD_DIM


def _gqa_prep_kernel(u_ref, cos_ref, sin_ref, gq_ref, gk_ref, mq_ref, mk_ref, q_ref, k_ref):
    u = u_ref[0].astype(jnp.float32)
    q = u[:, :_A_QW]
    k = u[:, _A_QW:_A_QW + _A_KW]
    qn = q * lax.rsqrt(_seg_mean_sq(q, mq_ref) + EPS) * gq_ref[...]
    kn = k * lax.rsqrt(_seg_mean_sq(k, mk_ref) + EPS) * gk_ref[...]
    cos = cos_ref[...]
    sin = sin_ref[...]
    reps = _A_QW // V7X_LANES
    qr = _rope(qn, jnp.concatenate([cos] * reps, axis=1), jnp.concatenate([sin] * reps, axis=1))
    kr = _rope(kn, cos, sin)
    q_ref[0] = (qr * (HEAD_DIM ** -0.5)).astype(q_ref.dtype)
    k_ref[0] = kr.astype(k_ref.dtype)


def _gqa_prep(ua, cos_t, sin_t, g_q, g_k):
    B, NTOK, W = ua.shape
    tm = TOKEN_TILE
    gq = jnp.tile(g_q, A_HEADS).reshape(1, _A_QW)
    gk = jnp.tile(g_k, A_KV_HEADS).reshape(1, _A_KW)
    mq = _seg_mean_matrix(_A_QW, HEAD_DIM)
    mk = _seg_mean_matrix(_A_KW, HEAD_DIM)
    const = lambda shape: pl.BlockSpec(shape, lambda b, t: (0,) * len(shape))
    return pl.pallas_call(
        _gqa_prep_kernel, grid=(B, NTOK // tm),
        in_specs=[pl.BlockSpec((1, tm, W), lambda b, t: (b, t, 0)),
                  pl.BlockSpec((tm, V7X_LANES), lambda b, t: (t, 0)),
                  pl.BlockSpec((tm, V7X_LANES), lambda b, t: (t, 0)),
                  const((1, _A_QW)), const((1, _A_KW)), const((_A_QW, _A_QW)), const((_A_KW, _A_KW))],
        out_specs=[pl.BlockSpec((1, tm, _A_QW), lambda b, t: (b, t, 0)),
                   pl.BlockSpec((1, tm, _A_KW), lambda b, t: (b, t, 0))],
        out_shape=[jax.ShapeDtypeStruct((B, NTOK, _A_QW), jnp.bfloat16),
                   jax.ShapeDtypeStruct((B, NTOK, _A_KW), jnp.bfloat16)],
        compiler_params=_vmem_params(16 * tm * W * 4, ("parallel", "parallel")), name="gqa_prep")(
            ua, cos_t, sin_t, gq, gk, mq, mk)


def _dot_nt(a, b):
    return lax.dot_general(a, b, (((1,), (1,)), ((), ())), preferred_element_type=jnp.float32)


def _softmax_pv(scores, values):
    m = None
    for s in scores:
        mi = jnp.max(s, axis=-1, keepdims=True)
        m = mi if m is None else jnp.maximum(m, mi)
    acc, den = None, None
    for s, v in zip(scores, values):
        p = jnp.exp(s - m)
        li = jnp.sum(p, axis=-1, keepdims=True)
        oi = jnp.dot(p.astype(jnp.bfloat16), v, preferred_element_type=jnp.float32)
        acc = oi if acc is None else acc + oi
        den = li if den is None else den + li
    return acc / den


def _gqa_attn_kernel(q_ref, k_ref, v_ref, o_ref, *, seq, n_lat_tiles):
    t = pl.program_id(1)
    ntok = k_ref.shape[1]

    def attend(lo):
        outs = []
        for j in range(A_KV_HEADS):
            kv = slice(j * HEAD_DIM, (j + 1) * HEAD_DIM)
            kj = k_ref[0, lo:ntok, kv]
            vj = v_ref[0, lo:ntok, kv]
            for g in range(A_GROUP):
                h = j * A_GROUP + g
                qh = q_ref[0, :, h * HEAD_DIM:(h + 1) * HEAD_DIM]
                outs.append(_softmax_pv([_dot_nt(qh, kj)], [vj]))
        o_ref[0] = jnp.concatenate(outs, axis=-1).astype(o_ref.dtype)

    @pl.when(t < n_lat_tiles)
    def _():
        attend(0)

    @pl.when(t >= n_lat_tiles)
    def _():
        attend(seq)


def _gqa_attn(qn, kn, ua, seq):
    B, NTOK, _ = qn.shape
    tq = TOKEN_TILE
    v_block = (_A_QW + _A_KW) // _A_KW
    nbytes = 4 * NTOK * _A_KW * 2 + 4 * tq * _A_QW * 2 + 6 * tq * NTOK * 4
    return pl.pallas_call(
        functools.partial(_gqa_attn_kernel, seq=seq, n_lat_tiles=seq // tq), grid=(B, NTOK // tq),
        in_specs=[pl.BlockSpec((1, tq, _A_QW), lambda b, t: (b, t, 0)),
                  pl.BlockSpec((1, NTOK, _A_KW), lambda b, t: (b, 0, 0)),
                  pl.BlockSpec((1, NTOK, _A_KW), lambda b, t: (b, 0, v_block))],
        out_specs=pl.BlockSpec((1, tq, _A_QW), lambda b, t: (b, t, 0)),
        out_shape=jax.ShapeDtypeStruct((B, NTOK, _A_QW), jnp.bfloat16),
        compiler_params=_vmem_params(nbytes, ("parallel", "arbitrary")), name="gqa_attn")(qn, kn, ua)


_NA_W = NA_HEADS * HEAD_DIM
_NA_KBLOCK = 3 * TOKEN_TILE
_NA_MASKED = -1e30


def _na_bias_table(rpb, seq):
    rows = seq // GRID_W
    rows_per_tile = TOKEN_TILE // GRID_W
    n_tiles = rows // rows_per_tile
    assert TOKEN_TILE % GRID_W == 0 and rows % rows_per_tile == 0 and n_tiles >= 3 and rows >= NA_WIN_R
    assert rows_per_tile + NA_WIN_R <= _NA_KBLOCK // GRID_W
    key_rows = _NA_KBLOCK // GRID_W
    n_r, n_c = 2 * NA_WIN_R - 1, 2 * NA_WIN_C - 1
    col = np.arange(GRID_W)
    cidx = np.clip(col[None, :] - col[:, None] + NA_WIN_C - 1, 0, n_c - 1)
    c_hot = (np.arange(n_c)[:, None, None] == cidx[None]).astype(np.float32)
    cs = np.clip(col - NA_WIN_C // 2, 0, GRID_W - NA_WIN_C)
    c_ok = (col[None, :] >= cs[:, None]) & (col[None, :] < cs[:, None] + NA_WIN_C)
    tables = []
    for t in (0, 1, n_tiles - 1):
        qr = t * rows_per_tile + np.arange(rows_per_tile)
        kr = int(np.clip(t - 1, 0, n_tiles - 3)) * rows_per_tile + np.arange(key_rows)
        rs = np.clip(qr - NA_WIN_R // 2, 0, rows - NA_WIN_R)
        r_ok = (kr[None, :] >= rs[:, None]) & (kr[None, :] < rs[:, None] + NA_WIN_R)
        ridx = np.clip(kr[None, :] - qr[:, None] + NA_WIN_R - 1, 0, n_r - 1)
        r_hot = (np.arange(n_r)[:, None, None] == ridx[None]).astype(np.float32)
        b = jnp.einsum('rak,hrc,cqj->haqkj', r_hot, rpb, c_hot, precision=lax.Precision.HIGHEST)
        valid = r_ok[:, None, :, None] & c_ok[None, :, None, :]
        tables.append(jnp.where(valid[None], b, _NA_MASKED).reshape(NA_HEADS, TOKEN_TILE, _NA_KBLOCK))
    return jnp.stack(tables)


def _na_attn_kernel(q_ref, k_ref, v_ref, bias_ref, o_ref, *, seq, n_lat_tiles):
    t = pl.program_id(1)
    ntok = k_ref.shape[1]
    scale = HEAD_DIM ** -0.5

    @pl.when(t < n_lat_tiles)
    def _():
        k_off = pl.multiple_of(jnp.clip(t - 1, 0, n_lat_tiles - 3) * TOKEN_TILE, TOKEN_TILE)
        outs = []
        for h in range(NA_HEADS):
            hs = slice(h * HEAD_DIM, (h + 1) * HEAD_DIM)
            qh = q_ref[0, :, hs] * scale
            s_nb = _dot_nt(qh, k_ref[0, pl.ds(k_off, _NA_KBLOCK), hs]) + bias_ref[0, h]
            s_ctx = _dot_nt(qh, k_ref[0, seq:ntok, hs])
            outs.append(_softmax_pv([s_nb, s_ctx], [v_ref[0, pl.ds(k_off, _NA_KBLOCK), hs], v_ref[0, seq:ntok, hs]]))
        o_ref[0] = jnp.concatenate(outs, axis=-1).astype(o_ref.dtype)

    @pl.when(t >= n_lat_tiles)
    def _():
        outs = []
        for h in range(NA_HEADS):
            hs = slice(h * HEAD_DIM, (h + 1) * HEAD_DIM)
            qh = q_ref[0, :, hs] * scale
            outs.append(_softmax_pv([_dot_nt(qh, k_ref[0, seq:ntok, hs])], [v_ref[0, seq:ntok, hs]]))
        o_ref[0] = jnp.concatenate(outs, axis=-1).astype(o_ref.dtype)


def _na_attn(ud, bias, seq):
    B, NTOK, _ = ud.shape
    tq = TOKEN_TILE
    n_lat_tiles = seq // tq
    nbytes = (4 * NTOK * _NA_W * 2 + 4 * tq * _NA_W * 2 + 2 * NA_HEADS * tq * _NA_KBLOCK * 4
              + 8 * tq * (_NA_KBLOCK + NTOK - seq) * 4)
    return pl.pallas_call(
        functools.partial(_na_attn_kernel, seq=seq, n_lat_tiles=n_lat_tiles), grid=(B, NTOK // tq),
        in_specs=[pl.BlockSpec((1, tq, _NA_W), lambda b, t: (b, t, 0)),
                  pl.BlockSpec((1, NTOK, _NA_W), lambda b, t: (b, 0, 1)),
                  pl.BlockSpec((1, NTOK, _NA_W), lambda b, t: (b, 0, 2)),
                  pl.BlockSpec((1, NA_HEADS, tq, _NA_KBLOCK),
                               lambda b, t: (jnp.where(t == 0, 0, jnp.where(t >= n_lat_tiles - 1, 2, 1)), 0, 0, 0))],
        out_specs=pl.BlockSpec((1, tq, _NA_W), lambda b, t: (b, t, 0)),
        out_shape=jax.ShapeDtypeStruct((B, NTOK, _NA_W), jnp.bfloat16),
        compiler_params=_vmem_params(nbytes, ("parallel", "arbitrary")), name="na_attn")(ud, ud, ud, bias)


_HALO = 8


def _dwconv_tile(prev8, cur, next8, w, row0, seq, ntok):
    x = jnp.concatenate([prev8, cur, next8], axis=0)
    n = x.shape[0]
    t = row0 - _HALO + lax.broadcasted_iota(jnp.int32, (n, 1), 0)
    keep = lambda hit: jnp.where(hit, 0.0, 1.0)
    first = keep(t == 0) * keep(t == seq)
    last1 = keep(t == seq - 1) * keep(t == ntok - 1)
    last2 = last1 * keep(t == seq - 2) * keep(t == ntok - 2)
    xm1 = pltpu.roll(x, 1, axis=0) * first
    xp1 = pltpu.roll(x, n - 1, axis=0) * last1
    xp2 = pltpu.roll(x, n - 2, axis=0) * last2
    y = w[0:1] * xm1 + w[1:2] * x + w[2:3] * xp1 + w[3:4] * xp2
    return y[_HALO:n - _HALO]


def _softplus(z):
    return jnp.log1p(jnp.exp(-jnp.abs(z))) + jnp.maximum(z, 0.0)


_LRU_CB = V7X_LANES


def _scan_tile(a, b, h_prev, reverse):
    n = a.shape[0]
    row = lax.broadcasted_iota(jnp.int32, (n, 1), 0)
    d = 1
    while d < n:
        if reverse:
            inside = row < n - d
            a_sh = pltpu.roll(a, n - d, axis=0)
            b_sh = pltpu.roll(b, n - d, axis=0)
        else:
            inside = row >= d
            a_sh = pltpu.roll(a, d, axis=0)
            b_sh = pltpu.roll(b, d, axis=0)
        b = jnp.where(inside, a * b_sh, 0.0) + b
        a = jnp.where(inside, a * a_sh, a)
        d *= 2
    h = a * h_prev + b
    return h, (h[0:1] if reverse else h[n - 1:n])


def _rglru_kernel(x_ref, y_ref, cw_ref, cb_ref, lam_ref, wr_ref, br_ref, wi_ref, bi_ref, o_ref, hf_ref,
                  *, seq, ntok):
    tm = TOKEN_TILE
    n_tiles, n_lat = ntok // tm, seq // tm
    n_ctx = n_tiles - n_lat

    def coeffs(i, d):
        r0 = pl.multiple_of(i * tm, tm)
        p0 = pl.multiple_of(jnp.maximum(r0 - _HALO, 0), _HALO)
        n0 = pl.multiple_of(jnp.minimum(r0 + tm, ntok - _HALO), _HALO)
        xc = _dwconv_tile(x_ref[0, pl.ds(p0, _HALO), :], x_ref[0, pl.ds(r0, tm), :], x_ref[0, pl.ds(n0, _HALO), :],
                          cw_ref[...], r0, seq, ntok) + cb_ref[...]
        xb = xc.astype(jnp.bfloat16)
        r = jax.nn.sigmoid(jnp.dot(xb, wr_ref[d, 0], preferred_element_type=jnp.float32) + br_ref[d:d + 1])
        g = jax.nn.sigmoid(jnp.dot(xb, wi_ref[d, 0], preferred_element_type=jnp.float32) + bi_ref[d:d + 1])
        a = jnp.exp(-LRU_C * r * _softplus(-lam_ref[d:d + 1]))
        return r0, a, jnp.sqrt(1.0 - a * a) * (g * xc)

    def fwd(k, h):
        i = jnp.where(k < n_ctx, n_lat + k, k - n_ctx)
        r0, a, b = coeffs(i, 0)
        ht, h = _scan_tile(a, b, h, reverse=False)
        hf_ref[pl.ds(r0, tm), :] = ht
        return h

    def bwd(k, h):
        i = jnp.where(k < n_ctx, n_tiles - 1 - k, n_lat - 1 - (k - n_ctx))
        r0, a, b = coeffs(i, 1)
        ht, h = _scan_tile(a, b, h, reverse=True)
        yv = y_ref[0, pl.ds(r0, tm), :]
        o_ref[0, pl.ds(r0, tm), :] = ((hf_ref[pl.ds(r0, tm), :] + ht) * jax.nn.gelu(yv)).astype(o_ref.dtype)
        return h

    h0 = jnp.zeros((1, _LRU_CB), jnp.float32)
    lax.fori_loop(0, n_tiles, fwd, h0)
    lax.fori_loop(0, n_tiles, bwd, h0)


def _rglru(ub, seq, conv_w, conv_b, lam, w_r, b_r, w_i, b_i):
    B, NTOK, _ = ub.shape
    cb = _LRU_CB
    ncb = LRU_WIDTH // cb
    per = cb // LRU_BLOCK_W

    def block_diag(w):
        w = w.reshape(2, ncb, per, LRU_BLOCK_W, LRU_BLOCK_W)
        eye = jnp.eye(per, dtype=w.dtype)
        return jnp.einsum('dnpij,pq->dnpiqj', w, eye).reshape(2, ncb, cb, cb).astype(jnp.bfloat16)

    vec = lambda rows: pl.BlockSpec((rows, cb), lambda b, c: (0, c))
    wspec = pl.BlockSpec((2, 1, cb, cb), lambda b, c: (0, c, 0, 0))
    nbytes = 2 * (2 * NTOK * cb * 4 + NTOK * cb * 2) + NTOK * cb * 4 + 64 * TOKEN_TILE * cb * 4
    return pl.pallas_call(
        functools.partial(_rglru_kernel, seq=seq, ntok=NTOK), grid=(B, ncb),
        in_specs=[pl.BlockSpec((1, NTOK, cb), lambda b, c: (b, 0, c)),
                  pl.BlockSpec((1, NTOK, cb), lambda b, c: (b, 0, ncb + c)),
                  vec(CONV_W), vec(1), vec(2), wspec, vec(2), wspec, vec(2)],
        out_specs=pl.BlockSpec((1, NTOK, cb), lambda b, c: (b, 0, c)),
        out_shape=jax.ShapeDtypeStruct((B, NTOK, LRU_WIDTH), jnp.bfloat16),
        scratch_shapes=[pltpu.VMEM((NTOK, cb), jnp.float32)],
        compiler_params=_vmem_params(nbytes, ("parallel", "parallel")), name="rglru")(
            ub, ub, conv_w, conv_b.reshape(1, -1), lam, block_diag(w_r), b_r, block_diag(w_i), b_i)


_DN_W = DN_HEADS * DN_DK
_DN_NCH = TOKEN_TILE // DN_CHUNK
_DN_NDH = 2 * DN_HEADS


def _l2n(x):
    return x * lax.rsqrt(jnp.sum(x * x, axis=-1, keepdims=True) + EPS)


def _dot_f32(a, b):
    return jnp.dot(a, b, preferred_element_type=jnp.float32, precision=lax.Precision.HIGHEST)


def _bdot(a, b):
    return jnp.dot(a.astype(jnp.bfloat16), b.astype(jnp.bfloat16), preferred_element_type=jnp.float32)


def _diag_blocks(m):
    c = DN_CHUNK
    return jnp.concatenate([m[i * c:(i + 1) * c, i * c:(i + 1) * c] for i in range(_DN_NCH)], axis=0)


def _gdn_chunk_kernel(prev_ref, cur_ref, next_ref, ab_ref, abt_ref, cw_ref, al_ref, dt_ref, alt_ref, dtt_ref,
                      u_ref, w_ref, qg_ref, kdt_ref, qk_ref, gl_ref, *, seq, ntok):
    tm = TOKEN_TILE
    c = DN_CHUNK
    row0 = pl.program_id(1) * tm
    qkv = jax.nn.silu(_dwconv_tile(prev_ref[0], cur_ref[0], next_ref[0], cw_ref[...], row0, seq, ntok))

    ndh = _DN_NDH
    g_col = -jnp.exp(al_ref[...]) * _softplus(ab_ref[0, :, 0:ndh] + dt_ref[...])
    beta_col = jax.nn.sigmoid(ab_ref[0, :, ndh:2 * ndh])
    g_row = -jnp.exp(alt_ref[...]) * _softplus(abt_ref[0, 0:ndh, :] + dtt_ref[...])

    ri = lax.broadcasted_iota(jnp.int32, (tm, tm), 0)
    ci = lax.broadcasted_iota(jnp.int32, (tm, tm), 1)
    blk = lambda v, size: jnp.right_shift(v, size.bit_length() - 1)
    same = blk(ri, c) == blk(ci, c)
    one = lambda m: jnp.where(m, 1.0, 0.0)
    low = one(same) * one(ri >= ci)
    upp = one(same) * one(ri <= ci)
    low_s = one(same) * one(ri > ci)
    upp_s = one(same) * one(ri < ci)
    samef = one(same)
    eye = one(ri == ci)
    base = 8
    blk8 = one(blk(ri, base) == blk(ci, base))
    merge = ([], [])
    n = base
    while n < c:
        pair = one(blk(ri, 2 * n) == blk(ci, 2 * n))
        r_odd, c_odd = one((blk(ri, n) & 1) == 1), one((blk(ci, n) & 1) == 1)
        merge[0].append(pair * r_odd * (1.0 - c_odd))
        merge[1].append(pair * (1.0 - r_odd) * c_odd)
        n *= 2

    nh = DN_HEADS
    is_fwd_col = lax.broadcasted_iota(jnp.int32, (tm, ndh), 1) < nh
    is_fwd_row = lax.broadcasted_iota(jnp.int32, (ndh, tm), 0) < nh
    gc_col = jnp.where(is_fwd_col, _dot_f32(low, g_col), _dot_f32(upp, g_col))
    gc_row = jnp.where(is_fwd_row, _dot_f32(g_row, upp), _dot_f32(g_row, low))
    gsum_col = _dot_f32(samef, g_col)

    for h in range(nh):
        hs = slice(h * DN_DK, (h + 1) * DN_DK)
        qh = _l2n(qkv[:, h * DN_DK:(h + 1) * DN_DK]) * (DN_DK ** -0.5)
        kh = _l2n(qkv[:, _DN_W + h * DN_DK:_DN_W + (h + 1) * DN_DK])
        vh = qkv[:, 2 * _DN_W + h * DN_DV:2 * _DN_W + (h + 1) * DN_DV]
        kk = _dot_nt(kh.astype(jnp.bfloat16), kh.astype(jnp.bfloat16))
        qk = _dot_nt(qh.astype(jnp.bfloat16), kh.astype(jnp.bfloat16))
        for d in range(2):
            j = d * nh + h
            mask, mask_s = (low, low_s) if d == 0 else (upp, upp_s)
            gcc = gc_col[:, j:j + 1]
            diff = gcc - gc_row[j:j + 1, :]
            dec = mask * jnp.exp(mask * diff)
            beta = beta_col[:, j:j + 1]
            egc = jnp.exp(gcc)
            kb = kh * beta
            lm = mask_s * (beta * kk) * dec
            p = lm * blk8
            p2 = _bdot(p, p)
            tinv = eye - p
            tinv = tinv + _bdot(tinv, p2)
            tinv = tinv + _bdot(tinv, _bdot(p2, p2))
            for mm in merge[d]:
                tinv = tinv - _bdot(_bdot(tinv, lm * mm), tinv)
            x = _bdot(tinv, jnp.concatenate([vh * beta, kb * egc], axis=1))
            u_ref[0, d, :, hs] = x[:, :DN_DV]
            w_ref[0, d, :, hs] = x[:, DN_DV:].astype(w_ref.dtype)
            qg_ref[0, d, :, hs] = (qh * egc).astype(qg_ref.dtype)
            kd = kh * jnp.exp(gsum_col[:, j:j + 1] - gcc)
            kdt_ref[0, d, hs, :] = kd.T.astype(kdt_ref.dtype)
            qk_ref[0, d, :, h * c:(h + 1) * c] = _diag_blocks(qk * dec).astype(qk_ref.dtype)
            gl = jnp.exp(gsum_col[:, j:j + 1])
            for i in range(_DN_NCH):
                gl_ref[0, d, i, :, hs] = jnp.broadcast_to(gl[i * c:i * c + 8], (8, DN_DK))


def _gdn_chunk(uq, uab, uab_t, seq, conv_w, a_log, dt_bias):
    B, NTOK, W = uq.shape
    tm = TOKEN_TILE
    nt = NTOK // tm
    hb = tm // _HALO
    ndh = _DN_NDH
    al, dt = a_log.reshape(1, ndh), dt_bias.reshape(1, ndh)
    const = lambda shape: pl.BlockSpec(shape, lambda b, t: (0,) * len(shape))
    tile4 = lambda w: pl.BlockSpec((1, 2, tm, w), lambda b, t: (b, 0, t, 0))
    f32, bf16 = jnp.float32, jnp.bfloat16
    nbytes = 2 * (tm + 16) * W * 4 * 6 + 2 * 2 * tm * (_DN_W * 12 + 256 * 2) + 40 * tm * tm * 4
    return pl.pallas_call(
        functools.partial(_gdn_chunk_kernel, seq=seq, ntok=NTOK), grid=(B, nt),
        in_specs=[pl.BlockSpec((1, _HALO, W), lambda b, t: (b, jnp.maximum(t * hb - 1, 0), 0)),
                  pl.BlockSpec((1, tm, W), lambda b, t: (b, t, 0)),
                  pl.BlockSpec((1, _HALO, W), lambda b, t: (b, jnp.minimum((t + 1) * hb, nt * hb - 1), 0)),
                  pl.BlockSpec((1, tm, V7X_LANES), lambda b, t: (b, t, 0)),
                  pl.BlockSpec((1, V7X_LANES, tm), lambda b, t: (b, 0, t)),
                  const((CONV_W, W)), const((1, ndh)), const((1, ndh)), const((ndh, 1)), const((ndh, 1))],
        out_specs=[tile4(_DN_W), tile4(_DN_W), tile4(_DN_W),
                   pl.BlockSpec((1, 2, _DN_W, tm), lambda b, t: (b, 0, 0, t)),
                   tile4(DN_HEADS * DN_CHUNK),
                   pl.BlockSpec((1, 2, _DN_NCH, 8, _DN_W), lambda b, t: (b, 0, t, 0, 0))],
        out_shape=[jax.ShapeDtypeStruct((B, 2, NTOK, _DN_W), f32),
                   jax.ShapeDtypeStruct((B, 2, NTOK, _DN_W), bf16),
                   jax.ShapeDtypeStruct((B, 2, NTOK, _DN_W), bf16),
                   jax.ShapeDtypeStruct((B, 2, _DN_W, NTOK), bf16),
                   jax.ShapeDtypeStruct((B, 2, NTOK, DN_HEADS * DN_CHUNK), bf16),
                   jax.ShapeDtypeStruct((B, 2, NTOK // DN_CHUNK, 8, _DN_W), f32)],
        compiler_params=_vmem_params(nbytes, ("parallel", "parallel")), name="gdn_chunk")(
            uq, uq, uq, uab, uab_t, conv_w, al, dt, al.reshape(ndh, 1), dt.reshape(ndh, 1))


def _gdn_scan_kernel(*refs):
    n_in = 6
    fwd, bwd = refs[:n_in], refs[n_in:2 * n_in]
    of_ref, ob_ref, s_ref = refs[2 * n_in:]
    c = DN_CHUNK

    @pl.when(pl.program_id(1) == 0)
    def _():
        s_ref[...] = jnp.zeros_like(s_ref)

    chains = [(d, h) for d in range(2) for h in range(DN_HEADS)]
    states = [s_ref[d, h] for d, h in chains]
    for step in range(_DN_NCH):
        for n, (d, h) in enumerate(chains):
            u_ref, w_ref, qg_ref, kdt_ref, qk_ref, gl_ref = fwd if d == 0 else bwd
            o_ref = of_ref if d == 0 else ob_ref
            i = step if d == 0 else _DN_NCH - 1 - step
            rows = slice(i * c, (i + 1) * c)
            hs = slice(h * DN_DK, (h + 1) * DN_DK)
            s = states[n]
            sb = s.astype(jnp.bfloat16)
            vn = u_ref[0, 0, rows, hs] - jnp.dot(w_ref[0, 0, rows, hs], sb, preferred_element_type=jnp.float32)
            vb = vn.astype(jnp.bfloat16)
            o_ref[0, rows, hs] = (jnp.dot(qg_ref[0, 0, rows, hs], sb, preferred_element_type=jnp.float32)
                                  + jnp.dot(qk_ref[0, 0, rows, h * c:(h + 1) * c], vb,
                                            preferred_element_type=jnp.float32))
            states[n] = s * gl_ref[0, 0, i, 0:1, hs] + jnp.dot(kdt_ref[0, 0, hs, rows], vb,
                                                             preferred_element_type=jnp.float32)
    for n, (d, h) in enumerate(chains):
        s_ref[d, h] = states[n]


def _gdn_scan(parts, seq):
    u, w, qg, kdt, qk, gl = parts
    B, _, NTOK, _ = u.shape
    tm = TOKEN_TILE
    nt, n_lat = NTOK // tm, seq // tm
    n_ctx = nt - n_lat
    fwd_tile = lambda s: jnp.where(s < n_ctx, n_lat + s, s - n_ctx)
    bwd_tile = lambda s: jnp.where(s < n_ctx, nt - 1 - s, n_lat - 1 - (s - n_ctx))

    def specs(d, tile):
        row = lambda wd: pl.BlockSpec((1, 1, tm, wd), lambda b, s: (b, d, tile(s), 0))
        return [row(_DN_W), row(_DN_W), row(_DN_W),
                pl.BlockSpec((1, 1, _DN_W, tm), lambda b, s: (b, d, 0, tile(s))),
                row(DN_HEADS * DN_CHUNK),
                pl.BlockSpec((1, 1, _DN_NCH, 8, _DN_W), lambda b, s: (b, d, tile(s), 0, 0))]

    out = jax.ShapeDtypeStruct((B, NTOK, _DN_W), jnp.float32)
    return pl.pallas_call(
        _gdn_scan_kernel, grid=(B, nt),
        in_specs=specs(0, fwd_tile) + specs(1, bwd_tile),
        out_specs=[pl.BlockSpec((1, tm, _DN_W), lambda b, s: (b, fwd_tile(s), 0)),
                   pl.BlockSpec((1, tm, _DN_W), lambda b, s: (b, bwd_tile(s), 0))],
        out_shape=[out, out],
        scratch_shapes=[pltpu.VMEM((2, DN_HEADS, DN_DK, DN_DV), jnp.float32)],
        compiler_params=_vmem_params(64 * tm * _DN_W * 4, ("parallel", "arbitrary")), name="gdn_scan")(
            u, w, qg, kdt, qk, gl, u, w, qg, kdt, qk, gl)


def _gdn_out_kernel(of_ref, ob_ref, z_ref, g_ref, y_ref):
    outs = []
    for h in range(DN_HEADS):
        hs = slice(h * DN_DV, (h + 1) * DN_DV)
        o = of_ref[0, :, hs] + ob_ref[0, :, hs]
        on = o * lax.rsqrt(jnp.mean(o * o, axis=-1, keepdims=True) + EPS) * g_ref[...]
        outs.append(on * jax.nn.silu(z_ref[0, :, hs].astype(jnp.float32)))
    y_ref[0] = jnp.concatenate(outs, axis=-1).astype(y_ref.dtype)


def _gdn_out(o_f, o_b, z, g_out):
    B, NTOK, W = o_f.shape
    tm = TOKEN_TILE
    tok = pl.BlockSpec((1, tm, W), lambda b, t: (b, t, 0))
    return pl.pallas_call(
        _gdn_out_kernel, grid=(B, NTOK // tm),
        in_specs=[tok, tok, tok, pl.BlockSpec((1, DN_DV), lambda b, t: (0, 0))],
        out_specs=tok, out_shape=jax.ShapeDtypeStruct((B, NTOK, W), jnp.bfloat16),
        compiler_params=_vmem_params(16 * tm * W * 4, ("parallel", "parallel")), name="gdn_out")(
            o_f, o_b, z, g_out.reshape(1, DN_DV))


def kernel(x, c, ctx, c_ctx, w_mod, b_mod, g_norm1, g_norm2, w_in, a_gq, a_gk, b_conv_w, b_conv_b, b_lam,
           b_wr, b_br, b_wi, b_bi, c_conv_w, c_alog, c_dtb, c_gout, d_rpb, w_branch, w_out, w_router,
           b_router, w_gu, b_gu, w_dn, b_dn, g_final):
    B, S, D = x.shape
    L = ctx.shape[1]
    depth = w_mod.shape[0]
    NTOK = S + L
    assert S % TOKEN_TILE == 0 and L % TOKEN_TILE == 0
    n_lat_tiles = S // TOKEN_TILE
    col = {}
    off = 0
    for name, width in _in_splits(D):
        col[name] = (off, off + width)
        off += width
    bf16 = jnp.bfloat16

    def w_cols(w, names, pad_to=None):
        parts = [w[:, col[n][0]:col[n][1]] for n in names]
        out = parts[0] if len(parts) == 1 else jnp.concatenate(parts, axis=1)
        if pad_to is not None:
            out = jnp.pad(out, ((0, 0), (0, pad_to - out.shape[1])))
        return out.astype(bf16)

    cos_t, sin_t = _rope_tables(S, L)
    sc = jax.nn.silu(c)
    scc = jax.nn.silu(c_ctx)
    xa = jnp.concatenate([x, ctx], axis=1)

    for l in range(depth):
        mod = sc @ w_mod[l] + b_mod[l]
        mod_c = scc @ w_mod[l] + b_mod[l]
        mod_all = jnp.stack([mod.reshape(B, 6, D), jnp.broadcast_to(mod_c.reshape(1, 6, D), (B, 6, D))], axis=1)

        h = _norm_mod(xa, g_norm1[l], mod_all, n_lat_tiles, 0, 1)
        hf = h.reshape(B * NTOK, D)
        proj = lambda names, dtype, tn, pad_to=None: _matmul(
            hf, w_cols(w_in[l], names, pad_to), dtype, tn=tn).reshape(B, NTOK, -1)
        ua = proj(('a_q', 'a_k', 'a_v'), bf16, 768)
        ub = proj(('b_x', 'b_y'), jnp.float32, 512)
        ucq = proj(('c_q', 'c_k', 'c_v'), jnp.float32, 768)
        cz = proj(('c_z',), bf16, 512)
        uab = proj(('c_a', 'c_b'), jnp.float32, V7X_LANES, pad_to=V7X_LANES)
        ud = proj(('d_q', 'd_k', 'd_v'), bf16, 768)
        gate_logits = proj(('gate',), bf16, min(1024, N_BRANCH * D))

        qn, kn = _gqa_prep(ua, cos_t, sin_t, a_gq[l], a_gk[l])
        ya = _gqa_attn(qn, kn, ua, S)
        yd = _na_attn(ud, _na_bias_table(d_rpb[l], S), S)

        yb = _rglru(ub, S, b_conv_w[l], b_conv_b[l], b_lam[l], b_wr[l], b_br[l], b_wi[l], b_bi[l])
        o_f, o_b = _gdn_scan(_gdn_chunk(ucq, uab, jnp.swapaxes(uab, 1, 2), S, c_conv_w[l], c_alog[l], c_dtb[l]), S)
        yc = _gdn_out(o_f, o_b, cz, c_gout[l])
        ys = [ya, yb, yc, yd]

        xa = _merge(ys, gate_logits, w_branch[l].astype(bf16), w_out[l].astype(bf16), xa, mod_all, n_lat_tiles)
        h2, logits = _norm_mod(xa, g_norm2[l], mod_all, n_lat_tiles, 3, 4, router=(w_router[l], b_router[l]))
        f = _moe(h2.reshape(B * NTOK, D), logits.reshape(B * NTOK, -1),
                 w_gu[l].astype(bf16), b_gu[l], w_dn[l].astype(bf16), b_dn[l])
        gt2 = mod_all[:, :, 5]
        gt2_rows = jnp.concatenate([jnp.broadcast_to(gt2[:, 0:1], (B, S, D)),
                                    jnp.broadcast_to(gt2[:, 1:2], (B, L, D))], axis=1)
        xa = xa + gt2_rows * f.reshape(B, NTOK, D)

    return _final_norm(xa, g_final, S)
```

```python
import functools

import jax
import jax.numpy as jnp
import numpy as np
from jax import lax
from jax.experimental import pallas as pl
from jax.experimental.pallas import tpu as pltpu

GRID_W = 64
HEAD_DIM = 64
EPS = 1e-6

A_HEADS = 8
A_KV_HEADS = 2
A_GROUP = A_HEADS // A_KV_HEADS
A_BLOCK = 128
ROPE_THETA = 10000.0

LRU_WIDTH = 512
LRU_BLOCKS = 8
LRU_BLOCK_W = LRU_WIDTH // LRU_BLOCKS
LRU_C = 8.0

CONV_W = 4
CONV_PAD = ((CONV_W - 1) // 2, CONV_W // 2)

DN_HEADS = 4
DN_DK = 128
DN_DV = 128
DN_CHUNK = 64

NA_HEADS = 8
NA_WIN_R = 8
NA_WIN_C = 16

N_BRANCH = 4
BRANCH_W = 512

TOP_K = 4
SWIGLU_LIMIT = 7.0
SWIGLU_ALPHA = 1.702

V7X_LANES = 128
V7X_MXU_DIM = 256
V7X_VMEM_BYTES = 64 * 1024 * 1024

TOKEN_TILE = 256
MOE_TILE = 256


def _in_splits(d_model):
    return (
        ('a_q', A_HEADS * HEAD_DIM), ('a_k', A_KV_HEADS * HEAD_DIM), ('a_v', A_KV_HEADS * HEAD_DIM),
        ('b_x', LRU_WIDTH), ('b_y', LRU_WIDTH),
        ('c_q', DN_HEADS * DN_DK), ('c_k', DN_HEADS * DN_DK), ('c_v', DN_HEADS * DN_DV),
        ('c_z', DN_HEADS * DN_DV), ('c_a', 2 * DN_HEADS), ('c_b', 2 * DN_HEADS),
        ('d_q', NA_HEADS * HEAD_DIM), ('d_k', NA_HEADS * HEAD_DIM), ('d_v', NA_HEADS * HEAD_DIM),
        ('gate', N_BRANCH * d_model),
    )


def _vmem_params(nbytes, semantics):
    limit = int(min(max(2 * nbytes, 32 * 1024 * 1024), V7X_VMEM_BYTES - 8 * 1024 * 1024))
    return pltpu.CompilerParams(dimension_semantics=semantics, vmem_limit_bytes=limit)


def _norm_mod_kernel(x_ref, g_ref, mod_ref, h_ref, *, shift_idx, scale_idx):
    x = x_ref[0]
    y = x * lax.rsqrt(jnp.mean(x * x, axis=-1, keepdims=True) + EPS) * g_ref[...]
    shift = mod_ref[0, 0, shift_idx:shift_idx + 1, :]
    scale = mod_ref[0, 0, scale_idx:scale_idx + 1, :]
    h_ref[0] = (y * (1 + scale) + shift).astype(h_ref.dtype)


def _norm_mod_router_kernel(x_ref, g_ref, mod_ref, wr_ref, br_ref, h_ref, lg_ref, *, shift_idx, scale_idx):
    x = x_ref[0]
    y = x * lax.rsqrt(jnp.mean(x * x, axis=-1, keepdims=True) + EPS) * g_ref[...]
    shift = mod_ref[0, 0, shift_idx:shift_idx + 1, :]
    scale = mod_ref[0, 0, scale_idx:scale_idx + 1, :]
    h = y * (1 + scale) + shift
    h_ref[0] = h.astype(h_ref.dtype)
    lg_ref[0] = jnp.dot(h, wr_ref[...], preferred_element_type=jnp.float32,
                        precision=lax.Precision.HIGHEST) + br_ref[...]


def _norm_mod(xa, g, mod_all, n_lat_tiles, shift_idx, scale_idx, router=None):
    B, NTOK, D = xa.shape
    tm = TOKEN_TILE
    grid = (B, NTOK // tm)
    x_spec = pl.BlockSpec((1, tm, D), lambda b, t: (b, t, 0))
    g_spec = pl.BlockSpec((1, D), lambda b, t: (0, 0))
    mod_spec = pl.BlockSpec((1, 1, 6, D), lambda b, t: (b, jnp.where(t >= n_lat_tiles, 1, 0), 0, 0))
    h_spec = pl.BlockSpec((1, tm, D), lambda b, t: (b, t, 0))
    params = _vmem_params(8 * tm * D * 4, ("parallel", "parallel"))
    if router is None:
        return pl.pallas_call(
            functools.partial(_norm_mod_kernel, shift_idx=shift_idx, scale_idx=scale_idx),
            grid=grid, in_specs=[x_spec, g_spec, mod_spec], out_specs=h_spec,
            out_shape=jax.ShapeDtypeStruct((B, NTOK, D), jnp.bfloat16),
            compiler_params=params, name="norm_mod")(xa, g.reshape(1, D), mod_all)
    w_router, b_router = router
    E = w_router.shape[1]
    return pl.pallas_call(
        functools.partial(_norm_mod_router_kernel, shift_idx=shift_idx, scale_idx=scale_idx),
        grid=grid,
        in_specs=[x_spec, g_spec, mod_spec, pl.BlockSpec((D, E), lambda b, t: (0, 0)),
                  pl.BlockSpec((1, E), lambda b, t: (0, 0))],
        out_specs=[h_spec, pl.BlockSpec((1, tm, E), lambda b, t: (b, t, 0))],
        out_shape=[jax.ShapeDtypeStruct((B, NTOK, D), jnp.float32),
                   jax.ShapeDtypeStruct((B, NTOK, E), jnp.float32)],
        compiler_params=params, name="norm_mod_router")(xa, g.reshape(1, D), mod_all, w_router,
                                                         b_router.reshape(1, E))


def _mm_kernel(a_ref, w_ref, o_ref):
    o_ref[...] = jnp.dot(a_ref[...], w_ref[...], preferred_element_type=jnp.float32).astype(o_ref.dtype)


def _matmul(a, w, out_dtype, tm=512, tn=512):
    M, K = a.shape
    N = w.shape[1]
    assert M % tm == 0 and N % tn == 0, (M, N, tm, tn)
    nbytes = 2 * (tm * K * a.dtype.itemsize + K * tn * w.dtype.itemsize + tm * tn * 4)
    return pl.pallas_call(
        _mm_kernel, grid=(N // tn, M // tm),
        in_specs=[pl.BlockSpec((tm, K), lambda j, i: (i, 0)), pl.BlockSpec((K, tn), lambda j, i: (0, j))],
        out_specs=pl.BlockSpec((tm, tn), lambda j, i: (i, j)),
        out_shape=jax.ShapeDtypeStruct((M, N), out_dtype),
        compiler_params=_vmem_params(nbytes, ("parallel", "parallel")), name="matmul")(a, w)


def _merge_kernel(ya_ref, yb_ref, yc_ref, yd_ref, gl_ref, wb_ref, wo_ref, x_ref, mod_ref, o_ref, *, d_model):
    m = None
    for i, y_ref in enumerate((ya_ref, yb_ref, yc_ref, yd_ref)):
        gates = jax.nn.sigmoid(gl_ref[0, :, i * d_model:(i + 1) * d_model].astype(jnp.float32))
        p = gates * jnp.dot(y_ref[0], wb_ref[i], preferred_element_type=jnp.float32)
        m = p if m is None else m + p
    out = jnp.dot(m.astype(jnp.bfloat16), wo_ref[...], preferred_element_type=jnp.float32)
    o_ref[0] = x_ref[0] + mod_ref[0, 0, 2:3, :] * out


def _merge(ys, gate_logits, w_branch, w_out, xa, mod_all, n_lat_tiles):
    B, NTOK, D = xa.shape
    tm = TOKEN_TILE
    tok = lambda w: pl.BlockSpec((1, tm, w), lambda b, t: (b, t, 0))
    nbytes = (2 * (4 * tm * BRANCH_W * 2 + tm * 4 * D * 4 + 2 * tm * D * 4)
              + 2 * (N_BRANCH * BRANCH_W * D * 2 + D * D * 2) + 4 * tm * D * 4)
    return pl.pallas_call(
        functools.partial(_merge_kernel, d_model=D), grid=(B, NTOK // tm),
        in_specs=[tok(BRANCH_W)] * 4 + [
            tok(N_BRANCH * D),
            pl.BlockSpec((N_BRANCH, BRANCH_W, D), lambda b, t: (0, 0, 0)),
            pl.BlockSpec((D, D), lambda b, t: (0, 0)),
            tok(D),
            pl.BlockSpec((1, 1, 6, D), lambda b, t: (b, jnp.where(t >= n_lat_tiles, 1, 0), 0, 0))],
        out_specs=tok(D), out_shape=jax.ShapeDtypeStruct((B, NTOK, D), jnp.float32),
        compiler_params=_vmem_params(nbytes, ("parallel", "parallel")), name="merge")(
            *ys, gate_logits, w_branch, w_out, xa, mod_all)


def _moe_kernel(te_ref, nv_ref, x_ref, wgu_ref, bgu_ref, wdn_ref, bdn_ref, o_ref, *, d_expert):
    i = pl.program_id(0)

    @pl.when(i < nv_ref[0])
    def _():
        gu = jnp.dot(x_ref[...].astype(jnp.bfloat16), wgu_ref[0], preferred_element_type=jnp.float32) + bgu_ref[0]
        gate = jnp.minimum(gu[:, :d_expert], SWIGLU_LIMIT)
        up = jnp.clip(gu[:, d_expert:], -SWIGLU_LIMIT, SWIGLU_LIMIT)
        act = (up + 1) * gate * jax.nn.sigmoid(SWIGLU_ALPHA * gate)
        y = jnp.dot(act.astype(jnp.bfloat16), wdn_ref[0], preferred_element_type=jnp.float32) + bdn_ref[0]
        o_ref[...] = y.astype(o_ref.dtype)

    @pl.when(i >= nv_ref[0])
    def _():
        o_ref[...] = jnp.zeros_like(o_ref)


def _moe_experts(x_sorted, tile_expert, n_valid, w_gu, b_gu, w_dn, b_dn):
    P, D = x_sorted.shape
    E, _, DE2 = w_gu.shape
    DE = DE2 // 2
    tm = MOE_TILE
    grid_spec = pltpu.PrefetchScalarGridSpec(
        num_scalar_prefetch=2, grid=(P // tm,),
        in_specs=[pl.BlockSpec((tm, D), lambda i, te, nv: (i, 0)),
                  pl.BlockSpec((1, D, DE2), lambda i, te, nv: (te[i], 0, 0)),
                  pl.BlockSpec((1, 1, DE2), lambda i, te, nv: (te[i], 0, 0)),
                  pl.BlockSpec((1, DE, D), lambda i, te, nv: (te[i], 0, 0)),
                  pl.BlockSpec((1, 1, D), lambda i, te, nv: (te[i], 0, 0))],
        out_specs=pl.BlockSpec((tm, D), lambda i, te, nv: (i, 0)))
    nbytes = 2 * (tm * D * 4 + D * DE2 * 2 + DE * D * 2 + tm * D * 4) + tm * DE2 * 4 * 3
    return pl.pallas_call(
        functools.partial(_moe_kernel, d_expert=DE), grid_spec=grid_spec,
        out_shape=jax.ShapeDtypeStruct((P, D), jnp.float32),
        compiler_params=_vmem_params(nbytes, ("arbitrary",)), name="moe_experts")(
            tile_expert, n_valid, x_sorted, w_gu, b_gu.reshape(E, 1, DE2), w_dn, b_dn.reshape(E, 1, D))


_ROUTE_TILE = 256


def _moe_route_kernel(e_ref, base_ref, dest_ref, carry_ref):
    n_exp, n = base_ref.shape[0], e_ref.shape[2]

    @pl.when(pl.program_id(0) == 0)
    def _():
        carry_ref[...] = jnp.zeros_like(carry_ref)

    hot = jnp.where(lax.broadcasted_iota(jnp.int32, (n_exp, n), 0) == e_ref[0], 1.0, 0.0)
    earlier = jnp.where(lax.broadcasted_iota(jnp.int32, (n, n), 0) < lax.broadcasted_iota(jnp.int32, (n, n), 1),
                        1.0, 0.0)
    rank = jnp.dot(hot.astype(jnp.bfloat16), earlier.astype(jnp.bfloat16), preferred_element_type=jnp.float32)
    slot = jnp.sum(hot * (rank + carry_ref[...] + base_ref[...]), axis=0, keepdims=True)
    dest_ref[0] = slot.astype(jnp.int32)
    carry_ref[...] = carry_ref[...] + jnp.sum(hot, axis=1, keepdims=True)


def _moe_route(flat_e, pad_start):
    n_pairs = flat_e.shape[0]
    n_exp = pad_start.shape[0]
    n = _ROUTE_TILE
    assert n_pairs % n == 0 and n_pairs + n_exp * MOE_TILE < 2 ** 24
    dest = pl.pallas_call(
        _moe_route_kernel, grid=(n_pairs // n,),
        in_specs=[pl.BlockSpec((1, 1, n), lambda i: (i, 0, 0)), pl.BlockSpec((n_exp, 1), lambda i: (0, 0))],
        out_specs=pl.BlockSpec((1, 1, n), lambda i: (i, 0, 0)),
        out_shape=jax.ShapeDtypeStruct((n_pairs // n, 1, n), jnp.int32),
        scratch_shapes=[pltpu.VMEM((n_exp, 1), jnp.float32)],
        compiler_params=_vmem_params(16 * n * n * 4, ("arbitrary",)), name="moe_route")(
            flat_e.reshape(n_pairs // n, 1, n), pad_start.astype(jnp.float32).reshape(n_exp, 1))
    return dest.reshape(n_pairs)


_DISPATCH_ROWS = 256
_COMBINE_ROWS = 128


def _row_copy(src_ref, src_row, dst_ref, dst_row, sem):
    return pltpu.make_async_copy(src_ref.at[pl.ds(src_row, 1)], dst_ref.at[pl.ds(dst_row, 1)], sem)


def _moe_dispatch_kernel(dest_ref, h_ref, xs_in_ref, xs_ref, sems):
    del xs_in_ref
    i = pl.program_id(0)
    rows = _DISPATCH_ROWS

    def issue(r, carry):
        t = i * rows + r
        for k in range(TOP_K):
            _row_copy(h_ref, t, xs_ref, dest_ref[t * TOP_K + k], sems.at[i % 2]).start()
        return carry

    def drain(slot):
        def body(r, carry):
            for _ in range(TOP_K):
                _row_copy(h_ref, 0, xs_ref, 0, sems.at[slot]).wait()
            return carry
        lax.fori_loop(0, rows, body, 0)

    lax.fori_loop(0, rows, issue, 0, unroll=8)

    @pl.when(i > 0)
    def _():
        drain((i - 1) % 2)

    @pl.when(i == pl.num_programs(0) - 1)
    def _():
        drain(i % 2)


def _moe_dispatch(h2, dest, n_slots):
    T, D = h2.shape
    assert T % _DISPATCH_ROWS == 0
    grid_spec = pltpu.PrefetchScalarGridSpec(
        num_scalar_prefetch=1, grid=(T // _DISPATCH_ROWS,),
        in_specs=[pl.BlockSpec(memory_space=pl.ANY), pl.BlockSpec(memory_space=pl.ANY)],
        out_specs=pl.BlockSpec(memory_space=pl.ANY),
        scratch_shapes=[pltpu.SemaphoreType.DMA((2,))])
    return pl.pallas_call(
        _moe_dispatch_kernel, grid_spec=grid_spec, out_shape=jax.ShapeDtypeStruct((n_slots, D), h2.dtype),
        input_output_aliases={2: 0},
        compiler_params=pltpu.CompilerParams(dimension_semantics=("arbitrary",)), name="moe_dispatch")(
            dest, h2, jnp.zeros((n_slots, D), h2.dtype))


def _moe_combine_kernel(dest_ref, y_ref, w_ref, x_ref, mod_ref, o_ref, buf, sems):
    rows = _COMBINE_ROWS
    lin = pl.program_id(0) * pl.num_programs(1) + pl.program_id(1)
    n_steps = pl.num_programs(0) * pl.num_programs(1)

    def issue(step, slot):
        def body(r, carry):
            t = step * rows + r
            for k in range(TOP_K):
                pltpu.make_async_copy(y_ref.at[pl.ds(dest_ref[t * TOP_K + k], 1)], buf.at[slot, k, pl.ds(r, 1)],
                                      sems.at[slot]).start()
            return carry
        lax.fori_loop(0, rows, body, 0, unroll=8)

    @pl.when(lin == 0)
    def _():
        issue(0, 0)

    @pl.when(lin + 1 < n_steps)
    def _():
        issue(lin + 1, (lin + 1) % 2)

    slot = lin % 2

    def wait_row(r, carry):
        for k in range(TOP_K):
            pltpu.make_async_copy(y_ref.at[pl.ds(0, 1)], buf.at[slot, k, pl.ds(0, 1)], sems.at[slot]).wait()
        return carry
    lax.fori_loop(0, rows, wait_row, 0)

    f = None
    for k in range(TOP_K):
        term = w_ref[:, k:k + 1] * buf[slot, k]
        f = term if f is None else f + term
    o_ref[0] = x_ref[0] + mod_ref[0, 0, 5:6, :] * f


def _moe_combine(y_sorted, dest, top_w, xa, mod_all, seq):
    B, NTOK, D = xa.shape
    rows = _COMBINE_ROWS
    assert seq % rows == 0 and NTOK % rows == 0
    n_lat = seq // rows
    tpb = NTOK // rows
    grid_spec = pltpu.PrefetchScalarGridSpec(
        num_scalar_prefetch=1, grid=(B, tpb),
        in_specs=[pl.BlockSpec(memory_space=pl.ANY),
                  pl.BlockSpec((rows, TOP_K), lambda b, t, d: (b * tpb + t, 0)),
                  pl.BlockSpec((1, rows, D), lambda b, t, d: (b, t, 0)),
                  pl.BlockSpec((1, 1, 6, D), lambda b, t, d: (b, jnp.where(t >= n_lat, 1, 0), 0, 0))],
        out_specs=pl.BlockSpec((1, rows, D), lambda b, t, d: (b, t, 0)),
        scratch_shapes=[pltpu.VMEM((2, TOP_K, rows, D), jnp.float32), pltpu.SemaphoreType.DMA((2,))])
    nbytes = 2 * TOP_K * rows * D * 4 + 8 * rows * D * 4
    return pl.pallas_call(
        _moe_combine_kernel, grid_spec=grid_spec, out_shape=jax.ShapeDtypeStruct((B, NTOK, D), jnp.float32),
        compiler_params=_vmem_params(nbytes, ("arbitrary", "arbitrary")), name="moe_combine")(
            dest, y_sorted, top_w, xa, mod_all)


def _moe(h2, logits, w_gu, b_gu, w_dn, b_dn, xa, mod_all, seq):
    T, D = h2.shape
    E = logits.shape[1]
    tm = MOE_TILE
    top_v, top_i = lax.top_k(logits, TOP_K)
    top_w = jax.nn.softmax(top_v, axis=-1)
    flat_e = top_i.reshape(-1).astype(jnp.int32)
    n_pairs = T * TOP_K
    n_slots = n_pairs + E * tm
    counts = jnp.sum(jax.nn.one_hot(flat_e, E, dtype=jnp.int32), axis=0)
    padded = ((counts + tm - 1) // tm) * tm
    pad_end = jnp.cumsum(padded)
    n_valid = (pad_end[-1] // tm).astype(jnp.int32).reshape(1)
    tile_start = jnp.arange(n_slots // tm, dtype=jnp.int32) * tm
    tile_expert = jnp.minimum(jnp.searchsorted(pad_end, tile_start, side='right'), E - 1).astype(jnp.int32)
    last_expert = tile_expert[jnp.maximum(n_valid[0] - 1, 0)]
    tile_expert = jnp.where(tile_start < pad_end[-1], tile_expert, last_expert)
    dest = _moe_route(flat_e, pad_end - padded)
    x_sorted = _moe_dispatch(h2, dest, n_slots)
    y_sorted = _moe_experts(x_sorted, tile_expert, n_valid, w_gu, b_gu, w_dn, b_dn)
    return _moe_combine(y_sorted, dest, top_w, xa, mod_all, seq)


def _final_norm_kernel(x_ref, g_ref, o_ref):
    x = x_ref[0]
    o_ref[0] = x * lax.rsqrt(jnp.mean(x * x, axis=-1, keepdims=True) + EPS) * g_ref[...]


def _final_norm(xa, g, seq):
    B, _, D = xa.shape
    tm = TOKEN_TILE
    return pl.pallas_call(
        _final_norm_kernel, grid=(B, seq // tm),
        in_specs=[pl.BlockSpec((1, tm, D), lambda b, t: (b, t, 0)), pl.BlockSpec((1, D), lambda b, t: (0, 0))],
        out_specs=pl.BlockSpec((1, tm, D), lambda b, t: (b, t, 0)),
        out_shape=jax.ShapeDtypeStruct((B, seq, D), jnp.float32),
        compiler_params=_vmem_params(4 * tm * D * 4, ("parallel", "parallel")), name="final_norm")(
            xa, g.reshape(1, D))


def _seg_mean_matrix(width, seg):
    idx = np.arange(width) // seg
    return jnp.asarray((idx[:, None] == idx[None, :]).astype(np.float32) / seg, jnp.bfloat16)


def _seg_mean_sq(x, m_ref):
    sq = x * x
    hi = sq.astype(jnp.bfloat16)
    lo = (sq - hi.astype(jnp.float32)).astype(jnp.bfloat16)
    return (jnp.dot(hi, m_ref[...], preferred_element_type=jnp.float32)
            + jnp.dot(lo, m_ref[...], preferred_element_type=jnp.float32))


def _rope(x, cos, sin_signed):
    width = x.shape[1]
    half = HEAD_DIM // 2
    lane = lax.broadcasted_iota(jnp.int32, x.shape, 1)
    fwd = pltpu.roll(x, width - half, axis=1)
    bwd = pltpu.roll(x, half, axis=1)
    rot = jnp.where((lane & (HEAD_DIM - 1)) < half, fwd, bwd)
    return x * cos + rot * sin_signed


def _rope_tables(seq, n_ctx):
    t = jnp.arange(seq)
    row = (t // GRID_W).astype(jnp.float32)
    col = (t % GRID_W).astype(jnp.float32)
    n_freq = HEAD_DIM // 4
    inv = ROPE_THETA ** (-jnp.arange(n_freq, dtype=jnp.float32) / n_freq)
    ang = jnp.concatenate([row[:, None] * inv[None], col[:, None] * inv[None]], axis=-1)
    cos, sin = jnp.cos(ang), jnp.sin(ang)
    reps = V7X_LANES // HEAD_DIM
    cos_t = jnp.tile(jnp.concatenate([cos, cos], axis=-1), (1, reps))
    sin_t = jnp.tile(jnp.concatenate([-sin, sin], axis=-1), (1, reps))
    cos_t = jnp.concatenate([cos_t, jnp.ones((n_ctx, V7X_LANES), jnp.float32)], axis=0)
    sin_t = jnp.concatenate([sin_t, jnp.zeros((n_ctx, V7X_LANES), jnp.float32)], axis=0)
    return cos_t, sin_t


_A_QW = A_HEADS * HEAD_DIM
_A_KW = A_KV_HEADS * HEAD_DIM


def _gqa_prep_kernel(u_ref, cos_ref, sin_ref, gq_ref, gk_ref, mq_ref, mk_ref, q_ref, k_ref):
    u = u_ref[0].astype(jnp.float32)
    q = u[:, :_A_QW]
    k = u[:, _A_QW:_A_QW + _A_KW]
    qn = q * lax.rsqrt(_seg_mean_sq(q, mq_ref) + EPS) * gq_ref[...]
    kn = k * lax.rsqrt(_seg_mean_sq(k, mk_ref) + EPS) * gk_ref[...]
    cos = cos_ref[...]
    sin = sin_ref[...]
    reps = _A_QW // V7X_LANES
    qr = _rope(qn, jnp.concatenate([cos] * reps, axis=1), jnp.concatenate([sin] * reps, axis=1))
    kr = _rope(kn, cos, sin)
    q_ref[0] = (qr * (HEAD_DIM ** -0.5)).astype(q_ref.dtype)
    k_ref[0] = kr.astype(k_ref.dtype)


def _gqa_prep(ua, cos_t, sin_t, g_q, g_k):
    B, NTOK, W = ua.shape
    tm = TOKEN_TILE
    gq = jnp.tile(g_q, A_HEADS).reshape(1, _A_QW)
    gk = jnp.tile(g_k, A_KV_HEADS).reshape(1, _A_KW)
    mq = _seg_mean_matrix(_A_QW, HEAD_DIM)
    mk = _seg_mean_matrix(_A_KW, HEAD_DIM)
    const = lambda shape: pl.BlockSpec(shape, lambda b, t: (0,) * len(shape))
    return pl.pallas_call(
        _gqa_prep_kernel, grid=(B, NTOK // tm),
        in_specs=[pl.BlockSpec((1, tm, W), lambda b, t: (b, t, 0)),
                  pl.BlockSpec((tm, V7X_LANES), lambda b, t: (t, 0)),
                  pl.BlockSpec((tm, V7X_LANES), lambda b, t: (t, 0)),
                  const((1, _A_QW)), const((1, _A_KW)), const((_A_QW, _A_QW)), const((_A_KW, _A_KW))],
        out_specs=[pl.BlockSpec((1, tm, _A_QW), lambda b, t: (b, t, 0)),
                   pl.BlockSpec((1, tm, _A_KW), lambda b, t: (b, t, 0))],
        out_shape=[jax.ShapeDtypeStruct((B, NTOK, _A_QW), jnp.bfloat16),
                   jax.ShapeDtypeStruct((B, NTOK, _A_KW), jnp.bfloat16)],
        compiler_params=_vmem_params(16 * tm * W * 4, ("parallel", "parallel")), name="gqa_prep")(
            ua, cos_t, sin_t, gq, gk, mq, mk)


def _dot_nt(a, b):
    return lax.dot_general(a, b, (((1,), (1,)), ((), ())), preferred_element_type=jnp.float32)


def _softmax_pv(scores, values):
    m = None
    for s in scores:
        mi = jnp.max(s, axis=-1, keepdims=True)
        m = mi if m is None else jnp.maximum(m, mi)
    acc, den = None, None
    for s, v in zip(scores, values):
        p = jnp.exp(s - m)
        li = jnp.sum(p, axis=-1, keepdims=True)
        oi = jnp.dot(p.astype(jnp.bfloat16), v, preferred_element_type=jnp.float32)
        acc = oi if acc is None else acc + oi
        den = li if den is None else den + li
    return acc / den


def _gqa_attn_kernel(q_ref, k_ref, v_ref, o_ref, *, seq, n_lat_tiles):
    t = pl.program_id(1)
    ntok = k_ref.shape[1]

    def attend(lo):
        outs = []
        for j in range(A_KV_HEADS):
            kv = slice(j * HEAD_DIM, (j + 1) * HEAD_DIM)
            kj = k_ref[0, lo:ntok, kv]
            vj = v_ref[0, lo:ntok, kv]
            for g in range(A_GROUP):
                h = j * A_GROUP + g
                qh = q_ref[0, :, h * HEAD_DIM:(h + 1) * HEAD_DIM]
                outs.append(_softmax_pv([_dot_nt(qh, kj)], [vj]))
        o_ref[0] = jnp.concatenate(outs, axis=-1).astype(o_ref.dtype)

    @pl.when(t < n_lat_tiles)
    def _():
        attend(0)

    @pl.when(t >= n_lat_tiles)
    def _():
        attend(seq)


def _gqa_attn(qn, kn, ua, seq):
    B, NTOK, _ = qn.shape
    tq = TOKEN_TILE
    v_block = (_A_QW + _A_KW) // _A_KW
    nbytes = 4 * NTOK * _A_KW * 2 + 4 * tq * _A_QW * 2 + 6 * tq * NTOK * 4
    return pl.pallas_call(
        functools.partial(_gqa_attn_kernel, seq=seq, n_lat_tiles=seq // tq), grid=(B, NTOK // tq),
        in_specs=[pl.BlockSpec((1, tq, _A_QW), lambda b, t: (b, t, 0)),
                  pl.BlockSpec((1, NTOK, _A_KW), lambda b, t: (b, 0, 0)),
                  pl.BlockSpec((1, NTOK, _A_KW), lambda b, t: (b, 0, v_block))],
        out_specs=pl.BlockSpec((1, tq, _A_QW), lambda b, t: (b, t, 0)),
        out_shape=jax.ShapeDtypeStruct((B, NTOK, _A_QW), jnp.bfloat16),
        compiler_params=_vmem_params(nbytes, ("parallel", "arbitrary")), name="gqa_attn")(qn, kn, ua)


_NA_W = NA_HEADS * HEAD_DIM
_NA_KBLOCK = 3 * TOKEN_TILE
_NA_MASKED = -1e30


def _na_bias_table(rpb, seq):
    rows = seq // GRID_W
    rows_per_tile = TOKEN_TILE // GRID_W
    n_tiles = rows // rows_per_tile
    assert TOKEN_TILE % GRID_W == 0 and rows % rows_per_tile == 0 and n_tiles >= 3 and rows >= NA_WIN_R
    assert rows_per_tile + NA_WIN_R <= _NA_KBLOCK // GRID_W
    key_rows = _NA_KBLOCK // GRID_W
    n_r, n_c = 2 * NA_WIN_R - 1, 2 * NA_WIN_C - 1
    col = np.arange(GRID_W)
    cidx = np.clip(col[None, :] - col[:, None] + NA_WIN_C - 1, 0, n_c - 1)
    c_hot = (np.arange(n_c)[:, None, None] == cidx[None]).astype(np.float32)
    cs = np.clip(col - NA_WIN_C // 2, 0, GRID_W - NA_WIN_C)
    c_ok = (col[None, :] >= cs[:, None]) & (col[None, :] < cs[:, None] + NA_WIN_C)
    tables = []
    for t in (0, 1, n_tiles - 1):
        qr = t * rows_per_tile + np.arange(rows_per_tile)
        kr = int(np.clip(t - 1, 0, n_tiles - 3)) * rows_per_tile + np.arange(key_rows)
        rs = np.clip(qr - NA_WIN_R // 2, 0, rows - NA_WIN_R)
        r_ok = (kr[None, :] >= rs[:, None]) & (kr[None, :] < rs[:, None] + NA_WIN_R)
        ridx = np.clip(kr[None, :] - qr[:, None] + NA_WIN_R - 1, 0, n_r - 1)
        r_hot = (np.arange(n_r)[:, None, None] == ridx[None]).astype(np.float32)
        b = jnp.einsum('rak,hrc,cqj->haqkj', r_hot, rpb, c_hot, precision=lax.Precision.HIGHEST)
        valid = r_ok[:, None, :, None] & c_ok[None, :, None, :]
        tables.append(jnp.where(valid[None], b, _NA_MASKED).reshape(NA_HEADS, TOKEN_TILE, _NA_KBLOCK))
    return jnp.stack(tables)


def _na_attn_kernel(q_ref, k_ref, v_ref, bias_ref, o_ref, *, seq, n_lat_tiles):
    t = pl.program_id(1)
    ntok = k_ref.shape[1]
    scale = HEAD_DIM ** -0.5

    @pl.when(t < n_lat_tiles)
    def _():
        k_off = pl.multiple_of(jnp.clip(t - 1, 0, n_lat_tiles - 3) * TOKEN_TILE, TOKEN_TILE)
        outs = []
        for h in range(NA_HEADS):
            hs = slice(h * HEAD_DIM, (h + 1) * HEAD_DIM)
            qh = q_ref[0, :, hs] * scale
            s_nb = _dot_nt(qh, k_ref[0, pl.ds(k_off, _NA_KBLOCK), hs]) + bias_ref[0, h]
            s_ctx = _dot_nt(qh, k_ref[0, seq:ntok, hs])
            outs.append(_softmax_pv([s_nb, s_ctx], [v_ref[0, pl.ds(k_off, _NA_KBLOCK), hs], v_ref[0, seq:ntok, hs]]))
        o_ref[0] = jnp.concatenate(outs, axis=-1).astype(o_ref.dtype)

    @pl.when(t >= n_lat_tiles)
    def _():
        outs = []
        for h in range(NA_HEADS):
            hs = slice(h * HEAD_DIM, (h + 1) * HEAD_DIM)
            qh = q_ref[0, :, hs] * scale
            outs.append(_softmax_pv([_dot_nt(qh, k_ref[0, seq:ntok, hs])], [v_ref[0, seq:ntok, hs]]))
        o_ref[0] = jnp.concatenate(outs, axis=-1).astype(o_ref.dtype)


def _na_attn(ud, bias, seq):
    B, NTOK, _ = ud.shape
    tq = TOKEN_TILE
    n_lat_tiles = seq // tq
    nbytes = (4 * NTOK * _NA_W * 2 + 4 * tq * _NA_W * 2 + 2 * NA_HEADS * tq * _NA_KBLOCK * 4
              + 8 * tq * (_NA_KBLOCK + NTOK - seq) * 4)
    return pl.pallas_call(
        functools.partial(_na_attn_kernel, seq=seq, n_lat_tiles=n_lat_tiles), grid=(B, NTOK // tq),
        in_specs=[pl.BlockSpec((1, tq, _NA_W), lambda b, t: (b, t, 0)),
                  pl.BlockSpec((1, NTOK, _NA_W), lambda b, t: (b, 0, 1)),
                  pl.BlockSpec((1, NTOK, _NA_W), lambda b, t: (b, 0, 2)),
                  pl.BlockSpec((1, NA_HEADS, tq, _NA_KBLOCK),
                               lambda b, t: (jnp.where(t == 0, 0, jnp.where(t >= n_lat_tiles - 1, 2, 1)), 0, 0, 0))],
        out_specs=pl.BlockSpec((1, tq, _NA_W), lambda b, t: (b, t, 0)),
        out_shape=jax.ShapeDtypeStruct((B, NTOK, _NA_W), jnp.bfloat16),
        compiler_params=_vmem_params(nbytes, ("parallel", "arbitrary")), name="na_attn")(ud, ud, ud, bias)


_HALO = 8


def _dwconv_tile(prev8, cur, next8, w, row0, seq, ntok):
    x = jnp.concatenate([prev8, cur, next8], axis=0)
    n = x.shape[0]
    t = row0 - _HALO + lax.broadcasted_iota(jnp.int32, (n, 1), 0)
    keep = lambda hit: jnp.where(hit, 0.0, 1.0)
    first = keep(t == 0) * keep(t == seq)
    last1 = keep(t == seq - 1) * keep(t == ntok - 1)
    last2 = last1 * keep(t == seq - 2) * keep(t == ntok - 2)
    xm1 = pltpu.roll(x, 1, axis=0) * first
    xp1 = pltpu.roll(x, n - 1, axis=0) * last1
    xp2 = pltpu.roll(x, n - 2, axis=0) * last2
    y = w[0:1] * xm1 + w[1:2] * x + w[2:3] * xp1 + w[3:4] * xp2
    return y[_HALO:n - _HALO]


def _softplus(z):
    return jnp.log1p(jnp.exp(-jnp.abs(z))) + jnp.maximum(z, 0.0)


_LRU_CB = V7X_LANES


def _scan_tile(a, b, h_prev, reverse):
    n = a.shape[0]
    row = lax.broadcasted_iota(jnp.int32, (n, 1), 0)
    d = 1
    while d < n:
        if reverse:
            inside = row < n - d
            a_sh = pltpu.roll(a, n - d, axis=0)
            b_sh = pltpu.roll(b, n - d, axis=0)
        else:
            inside = row >= d
            a_sh = pltpu.roll(a, d, axis=0)
            b_sh = pltpu.roll(b, d, axis=0)
        b = jnp.where(inside, a * b_sh, 0.0) + b
        a = jnp.where(inside, a * a_sh, a)
        d *= 2
    h = a * h_prev + b
    return h, (h[0:1] if reverse else h[n - 1:n])


def _rglru_kernel(x_ref, y_ref, cw_ref, cb_ref, lam_ref, wr_ref, br_ref, wi_ref, bi_ref, o_ref, hf_ref,
                  *, seq, ntok):
    tm = TOKEN_TILE
    n_tiles, n_lat = ntok // tm, seq // tm
    n_ctx = n_tiles - n_lat

    def coeffs(i, d):
        r0 = pl.multiple_of(i * tm, tm)
        p0 = pl.multiple_of(jnp.maximum(r0 - _HALO, 0), _HALO)
        n0 = pl.multiple_of(jnp.minimum(r0 + tm, ntok - _HALO), _HALO)
        xc = _dwconv_tile(x_ref[0, pl.ds(p0, _HALO), :], x_ref[0, pl.ds(r0, tm), :], x_ref[0, pl.ds(n0, _HALO), :],
                          cw_ref[...], r0, seq, ntok) + cb_ref[...]
        xb = xc.astype(jnp.bfloat16)
        r = jax.nn.sigmoid(jnp.dot(xb, wr_ref[d, 0], preferred_element_type=jnp.float32) + br_ref[d:d + 1])
        g = jax.nn.sigmoid(jnp.dot(xb, wi_ref[d, 0], preferred_element_type=jnp.float32) + bi_ref[d:d + 1])
        a = jnp.exp(-LRU_C * r * _softplus(-lam_ref[d:d + 1]))
        return r0, a, jnp.sqrt(1.0 - a * a) * (g * xc)

    def fwd(k, h):
        i = jnp.where(k < n_ctx, n_lat + k, k - n_ctx)
        r0, a, b = coeffs(i, 0)
        ht, h = _scan_tile(a, b, h, reverse=False)
        hf_ref[pl.ds(r0, tm), :] = ht
        return h

    def bwd(k, h):
        i = jnp.where(k < n_ctx, n_tiles - 1 - k, n_lat - 1 - (k - n_ctx))
        r0, a, b = coeffs(i, 1)
        ht, h = _scan_tile(a, b, h, reverse=True)
        yv = y_ref[0, pl.ds(r0, tm), :]
        o_ref[0, pl.ds(r0, tm), :] = ((hf_ref[pl.ds(r0, tm), :] + ht) * jax.nn.gelu(yv)).astype(o_ref.dtype)
        return h

    h0 = jnp.zeros((1, _LRU_CB), jnp.float32)
    lax.fori_loop(0, n_tiles, fwd, h0)
    lax.fori_loop(0, n_tiles, bwd, h0)


def _rglru(ub, seq, conv_w, conv_b, lam, w_r, b_r, w_i, b_i):
    B, NTOK, _ = ub.shape
    cb = _LRU_CB
    ncb = LRU_WIDTH // cb
    per = cb // LRU_BLOCK_W

    def block_diag(w):
        w = w.reshape(2, ncb, per, LRU_BLOCK_W, LRU_BLOCK_W)
        eye = jnp.eye(per, dtype=w.dtype)
        return jnp.einsum('dnpij,pq->dnpiqj', w, eye).reshape(2, ncb, cb, cb).astype(jnp.bfloat16)

    vec = lambda rows: pl.BlockSpec((rows, cb), lambda b, c: (0, c))
    wspec = pl.BlockSpec((2, 1, cb, cb), lambda b, c: (0, c, 0, 0))
    nbytes = 2 * (2 * NTOK * cb * 4 + NTOK * cb * 2) + NTOK * cb * 4 + 64 * TOKEN_TILE * cb * 4
    return pl.pallas_call(
        functools.partial(_rglru_kernel, seq=seq, ntok=NTOK), grid=(B, ncb),
        in_specs=[pl.BlockSpec((1, NTOK, cb), lambda b, c: (b, 0, c)),
                  pl.BlockSpec((1, NTOK, cb), lambda b, c: (b, 0, ncb + c)),
                  vec(CONV_W), vec(1), vec(2), wspec, vec(2), wspec, vec(2)],
        out_specs=pl.BlockSpec((1, NTOK, cb), lambda b, c: (b, 0, c)),
        out_shape=jax.ShapeDtypeStruct((B, NTOK, LRU_WIDTH), jnp.bfloat16),
        scratch_shapes=[pltpu.VMEM((NTOK, cb), jnp.float32)],
        compiler_params=_vmem_params(nbytes, ("parallel", "parallel")), name="rglru")(
            ub, ub, conv_w, conv_b.reshape(1, -1), lam, block_diag(w_r), b_r, block_diag(w_i), b_i)


_DN_W = DN_HEADS * DN_DK
_DN_NCH = TOKEN_TILE // DN_CHUNK
_DN_NDH = 2 * DN_HEADS


def _l2n(x):
    return x * lax.rsqrt(jnp.sum(x * x, axis=-1, keepdims=True) + EPS)


def _dot_f32(a, b):
    return jnp.dot(a, b, preferred_element_type=jnp.float32, precision=lax.Precision.HIGHEST)


def _bdot(a, b):
    return jnp.dot(a.astype(jnp.bfloat16), b.astype(jnp.bfloat16), preferred_element_type=jnp.float32)


def _diag_blocks(m):
    c = DN_CHUNK
    return jnp.concatenate([m[i * c:(i + 1) * c, i * c:(i + 1) * c] for i in range(_DN_NCH)], axis=0)


def _gdn_chunk_kernel(prev_ref, cur_ref, next_ref, ab_ref, abt_ref, cw_ref, al_ref, dt_ref, alt_ref, dtt_ref,
                      u_ref, w_ref, qg_ref, kdt_ref, qk_ref, gl_ref, *, seq, ntok):
    tm = TOKEN_TILE
    c = DN_CHUNK
    row0 = pl.program_id(1) * tm
    qkv = jax.nn.silu(_dwconv_tile(prev_ref[0], cur_ref[0], next_ref[0], cw_ref[...], row0, seq, ntok))

    ndh = _DN_NDH
    g_col = -jnp.exp(al_ref[...]) * _softplus(ab_ref[0, :, 0:ndh] + dt_ref[...])
    beta_col = jax.nn.sigmoid(ab_ref[0, :, ndh:2 * ndh])
    g_row = -jnp.exp(alt_ref[...]) * _softplus(abt_ref[0, 0:ndh, :] + dtt_ref[...])

    ri = lax.broadcasted_iota(jnp.int32, (tm, tm), 0)
    ci = lax.broadcasted_iota(jnp.int32, (tm, tm), 1)
    blk = lambda v, size: jnp.right_shift(v, size.bit_length() - 1)
    same = blk(ri, c) == blk(ci, c)
    one = lambda m: jnp.where(m, 1.0, 0.0)
    low = one(same) * one(ri >= ci)
    upp = one(same) * one(ri <= ci)
    low_s = one(same) * one(ri > ci)
    upp_s = one(same) * one(ri < ci)
    samef = one(same)
    eye = one(ri == ci)
    base = 8
    blk8 = one(blk(ri, base) == blk(ci, base))
    merge = ([], [])
    n = base
    while n < c:
        pair = one(blk(ri, 2 * n) == blk(ci, 2 * n))
        r_odd, c_odd = one((blk(ri, n) & 1) == 1), one((blk(ci, n) & 1) == 1)
        merge[0].append(pair * r_odd * (1.0 - c_odd))
        merge[1].append(pair * (1.0 - r_odd) * c_odd)
        n *= 2

    nh = DN_HEADS
    is_fwd_col = lax.broadcasted_iota(jnp.int32, (tm, ndh), 1) < nh
    is_fwd_row = lax.broadcasted_iota(jnp.int32, (ndh, tm), 0) < nh
    gc_col = jnp.where(is_fwd_col, _dot_f32(low, g_col), _dot_f32(upp, g_col))
    gc_row = jnp.where(is_fwd_row, _dot_f32(g_row, upp), _dot_f32(g_row, low))
    gsum_col = _dot_f32(samef, g_col)

    for h in range(nh):
        hs = slice(h * DN_DK, (h + 1) * DN_DK)
        qh = _l2n(qkv[:, h * DN_DK:(h + 1) * DN_DK]) * (DN_DK ** -0.5)
        kh = _l2n(qkv[:, _DN_W + h * DN_DK:_DN_W + (h + 1) * DN_DK])
        vh = qkv[:, 2 * _DN_W + h * DN_DV:2 * _DN_W + (h + 1) * DN_DV]
        kk = _dot_nt(kh.astype(jnp.bfloat16), kh.astype(jnp.bfloat16))
        qk = _dot_nt(qh.astype(jnp.bfloat16), kh.astype(jnp.bfloat16))
        for d in range(2):
            j = d * nh + h
            mask, mask_s = (low, low_s) if d == 0 else (upp, upp_s)
            gcc = gc_col[:, j:j + 1]
            diff = gcc - gc_row[j:j + 1, :]
            dec = mask * jnp.exp(mask * diff)
            beta = beta_col[:, j:j + 1]
            egc = jnp.exp(gcc)
            kb = kh * beta
            lm = mask_s * (beta * kk) * dec
            p = lm * blk8
            p2 = _bdot(p, p)
            tinv = eye - p
            tinv = tinv + _bdot(tinv, p2)
            tinv = tinv + _bdot(tinv, _bdot(p2, p2))
            for mm in merge[d]:
                tinv = tinv - _bdot(_bdot(tinv, lm * mm), tinv)
            x = _bdot(tinv, jnp.concatenate([vh * beta, kb * egc], axis=1))
            u_ref[0, d, :, hs] = x[:, :DN_DV]
            w_ref[0, d, :, hs] = x[:, DN_DV:].astype(w_ref.dtype)
            qg_ref[0, d, :, hs] = (qh * egc).astype(qg_ref.dtype)
            kd = kh * jnp.exp(gsum_col[:, j:j + 1] - gcc)
            kdt_ref[0, d, hs, :] = kd.T.astype(kdt_ref.dtype)
            qk_ref[0, d, :, h * c:(h + 1) * c] = _diag_blocks(qk * dec).astype(qk_ref.dtype)
            gl = jnp.exp(gsum_col[:, j:j + 1])
            for i in range(_DN_NCH):
                gl_ref[0, d, i, :, hs] = jnp.broadcast_to(gl[i * c:i * c + 8], (8, DN_DK))


def _gdn_chunk(uq, uab, uab_t, seq, conv_w, a_log, dt_bias):
    B, NTOK, W = uq.shape
    tm = TOKEN_TILE
    nt = NTOK // tm
    hb = tm // _HALO
    ndh = _DN_NDH
    al, dt = a_log.reshape(1, ndh), dt_bias.reshape(1, ndh)
    const = lambda shape: pl.BlockSpec(shape, lambda b, t: (0,) * len(shape))
    tile4 = lambda w: pl.BlockSpec((1, 2, tm, w), lambda b, t: (b, 0, t, 0))
    f32, bf16 = jnp.float32, jnp.bfloat16
    nbytes = 2 * (tm + 16) * W * 4 * 6 + 2 * 2 * tm * (_DN_W * 12 + 256 * 2) + 40 * tm * tm * 4
    return pl.pallas_call(
        functools.partial(_gdn_chunk_kernel, seq=seq, ntok=NTOK), grid=(B, nt),
        in_specs=[pl.BlockSpec((1, _HALO, W), lambda b, t: (b, jnp.maximum(t * hb - 1, 0), 0)),
                  pl.BlockSpec((1, tm, W), lambda b, t: (b, t, 0)),
                  pl.BlockSpec((1, _HALO, W), lambda b, t: (b, jnp.minimum((t + 1) * hb, nt * hb - 1), 0)),
                  pl.BlockSpec((1, tm, V7X_LANES), lambda b, t: (b, t, 0)),
                  pl.BlockSpec((1, V7X_LANES, tm), lambda b, t: (b, 0, t)),
                  const((CONV_W, W)), const((1, ndh)), const((1, ndh)), const((ndh, 1)), const((ndh, 1))],
        out_specs=[tile4(_DN_W), tile4(_DN_W), tile4(_DN_W),
                   pl.BlockSpec((1, 2, _DN_W, tm), lambda b, t: (b, 0, 0, t)),
                   tile4(DN_HEADS * DN_CHUNK),
                   pl.BlockSpec((1, 2, _DN_NCH, 8, _DN_W), lambda b, t: (b, 0, t, 0, 0))],
        out_shape=[jax.ShapeDtypeStruct((B, 2, NTOK, _DN_W), f32),
                   jax.ShapeDtypeStruct((B, 2, NTOK, _DN_W), bf16),
                   jax.ShapeDtypeStruct((B, 2, NTOK, _DN_W), bf16),
                   jax.ShapeDtypeStruct((B, 2, _DN_W, NTOK), bf16),
                   jax.ShapeDtypeStruct((B, 2, NTOK, DN_HEADS * DN_CHUNK), bf16),
                   jax.ShapeDtypeStruct((B, 2, NTOK // DN_CHUNK, 8, _DN_W), f32)],
        compiler_params=_vmem_params(nbytes, ("parallel", "parallel")), name="gdn_chunk")(
            uq, uq, uq, uab, uab_t, conv_w, al, dt, al.reshape(ndh, 1), dt.reshape(ndh, 1))


def _gdn_scan_kernel(*refs):
    n_in = 6
    fwd, bwd = refs[:n_in], refs[n_in:2 * n_in]
    of_ref, ob_ref, s_ref = refs[2 * n_in:]
    c = DN_CHUNK

    @pl.when(pl.program_id(1) == 0)
    def _():
        s_ref[...] = jnp.zeros_like(s_ref)

    chains = [(d, h) for d in range(2) for h in range(DN_HEADS)]
    states = [s_ref[d, h] for d, h in chains]
    for step in range(_DN_NCH):
        for n, (d, h) in enumerate(chains):
            u_ref, w_ref, qg_ref, kdt_ref, qk_ref, gl_ref = fwd if d == 0 else bwd
            o_ref = of_ref if d == 0 else ob_ref
            i = step if d == 0 else _DN_NCH - 1 - step
            rows = slice(i * c, (i + 1) * c)
            hs = slice(h * DN_DK, (h + 1) * DN_DK)
            s = states[n]
            sb = s.astype(jnp.bfloat16)
            vn = u_ref[0, 0, rows, hs] - jnp.dot(w_ref[0, 0, rows, hs], sb, preferred_element_type=jnp.float32)
            vb = vn.astype(jnp.bfloat16)
            o_ref[0, rows, hs] = (jnp.dot(qg_ref[0, 0, rows, hs], sb, preferred_element_type=jnp.float32)
                                  + jnp.dot(qk_ref[0, 0, rows, h * c:(h + 1) * c], vb,
                                            preferred_element_type=jnp.float32))
            states[n] = s * gl_ref[0, 0, i, 0:1, hs] + jnp.dot(kdt_ref[0, 0, hs, rows], vb,
                                                             preferred_element_type=jnp.float32)
    for n, (d, h) in enumerate(chains):
        s_ref[d, h] = states[n]


def _gdn_scan(parts, seq):
    u, w, qg, kdt, qk, gl = parts
    B, _, NTOK, _ = u.shape
    tm = TOKEN_TILE
    nt, n_lat = NTOK // tm, seq // tm
    n_ctx = nt - n_lat
    fwd_tile = lambda s: jnp.where(s < n_ctx, n_lat + s, s - n_ctx)
    bwd_tile = lambda s: jnp.where(s < n_ctx, nt - 1 - s, n_lat - 1 - (s - n_ctx))

    def specs(d, tile):
        row = lambda wd: pl.BlockSpec((1, 1, tm, wd), lambda b, s: (b, d, tile(s), 0))
        return [row(_DN_W), row(_DN_W), row(_DN_W),
                pl.BlockSpec((1, 1, _DN_W, tm), lambda b, s: (b, d, 0, tile(s))),
                row(DN_HEADS * DN_CHUNK),
                pl.BlockSpec((1, 1, _DN_NCH, 8, _DN_W), lambda b, s: (b, d, tile(s), 0, 0))]

    out = jax.ShapeDtypeStruct((B, NTOK, _DN_W), jnp.float32)
    return pl.pallas_call(
        _gdn_scan_kernel, grid=(B, nt),
        in_specs=specs(0, fwd_tile) + specs(1, bwd_tile),
        out_specs=[pl.BlockSpec((1, tm, _DN_W), lambda b, s: (b, fwd_tile(s), 0)),
                   pl.BlockSpec((1, tm, _DN_W), lambda b, s: (b, bwd_tile(s), 0))],
        out_shape=[out, out],
        scratch_shapes=[pltpu.VMEM((2, DN_HEADS, DN_DK, DN_DV), jnp.float32)],
        compiler_params=_vmem_params(64 * tm * _DN_W * 4, ("parallel", "arbitrary")), name="gdn_scan")(
            u, w, qg, kdt, qk, gl, u, w, qg, kdt, qk, gl)


def _gdn_out_kernel(of_ref, ob_ref, z_ref, g_ref, y_ref):
    outs = []
    for h in range(DN_HEADS):
        hs = slice(h * DN_DV, (h + 1) * DN_DV)
        o = of_ref[0, :, hs] + ob_ref[0, :, hs]
        on = o * lax.rsqrt(jnp.mean(o * o, axis=-1, keepdims=True) + EPS) * g_ref[...]
        outs.append(on * jax.nn.silu(z_ref[0, :, hs].astype(jnp.float32)))
    y_ref[0] = jnp.concatenate(outs, axis=-1).astype(y_ref.dtype)


def _gdn_out(o_f, o_b, z, g_out):
    B, NTOK, W = o_f.shape
    tm = TOKEN_TILE
    tok = pl.BlockSpec((1, tm, W), lambda b, t: (b, t, 0))
    return pl.pallas_call(
        _gdn_out_kernel, grid=(B, NTOK // tm),
        in_specs=[tok, tok, tok, pl.BlockSpec((1, DN_DV), lambda b, t: (0, 0))],
        out_specs=tok, out_shape=jax.ShapeDtypeStruct((B, NTOK, W), jnp.bfloat16),
        compiler_params=_vmem_params(16 * tm * W * 4, ("parallel", "parallel")), name="gdn_out")(
            o_f, o_b, z, g_out.reshape(1, DN_DV))


def kernel(x, c, ctx, c_ctx, w_mod, b_mod, g_norm1, g_norm2, w_in, a_gq, a_gk, b_conv_w, b_conv_b, b_lam,
           b_wr, b_br, b_wi, b_bi, c_conv_w, c_alog, c_dtb, c_gout, d_rpb, w_branch, w_out, w_router,
           b_router, w_gu, b_gu, w_dn, b_dn, g_final):
    B, S, D = x.shape
    L = ctx.shape[1]
    depth = w_mod.shape[0]
    NTOK = S + L
    assert S % TOKEN_TILE == 0 and L % TOKEN_TILE == 0
    n_lat_tiles = S // TOKEN_TILE
    col = {}
    off = 0
    for name, width in _in_splits(D):
        col[name] = (off, off + width)
        off += width
    bf16 = jnp.bfloat16

    def w_cols(w, names, pad_to=None):
        parts = [w[:, col[n][0]:col[n][1]] for n in names]
        out = parts[0] if len(parts) == 1 else jnp.concatenate(parts, axis=1)
        if pad_to is not None:
            out = jnp.pad(out, ((0, 0), (0, pad_to - out.shape[1])))
        return out.astype(bf16)

    cos_t, sin_t = _rope_tables(S, L)
    sc = jax.nn.silu(c)
    scc = jax.nn.silu(c_ctx)
    xa = jnp.concatenate([x, ctx], axis=1)

    for l in range(depth):
        mod = sc @ w_mod[l] + b_mod[l]
        mod_c = scc @ w_mod[l] + b_mod[l]
        mod_all = jnp.stack([mod.reshape(B, 6, D), jnp.broadcast_to(mod_c.reshape(1, 6, D), (B, 6, D))], axis=1)

        h = _norm_mod(xa, g_norm1[l], mod_all, n_lat_tiles, 0, 1)
        hf = h.reshape(B * NTOK, D)
        proj = lambda names, dtype, tn, pad_to=None: _matmul(
            hf, w_cols(w_in[l], names, pad_to), dtype, tn=tn).reshape(B, NTOK, -1)
        ua = proj(('a_q', 'a_k', 'a_v'), bf16, 768)
        ub = proj(('b_x', 'b_y'), jnp.float32, 512)
        ucq = proj(('c_q', 'c_k', 'c_v'), jnp.float32, 768)
        cz = proj(('c_z',), bf16, 512)
        uab = proj(('c_a', 'c_b'), jnp.float32, V7X_LANES, pad_to=V7X_LANES)
        ud = proj(('d_q', 'd_k', 'd_v'), bf16, 768)
        gate_logits = proj(('gate',), bf16, min(1024, N_BRANCH * D))

        qn, kn = _gqa_prep(ua, cos_t, sin_t, a_gq[l], a_gk[l])
        ya = _gqa_attn(qn, kn, ua, S)
        yd = _na_attn(ud, _na_bias_table(d_rpb[l], S), S)

        yb = _rglru(ub, S, b_conv_w[l], b_conv_b[l], b_lam[l], b_wr[l], b_br[l], b_wi[l], b_bi[l])
        o_f, o_b = _gdn_scan(_gdn_chunk(ucq, uab, jnp.swapaxes(uab, 1, 2), S, c_conv_w[l], c_alog[l], c_dtb[l]), S)
        yc = _gdn_out(o_f, o_b, cz, c_gout[l])
        ys = [ya, yb, yc, yd]

        xa = _merge(ys, gate_logits, w_branch[l].astype(bf16), w_out[l].astype(bf16), xa, mod_all, n_lat_tiles)
        h2, logits = _norm_mod(xa, g_norm2[l], mod_all, n_lat_tiles, 3, 4, router=(w_router[l], b_router[l]))
        xa = _moe(h2.reshape(B * NTOK, D), logits.reshape(B * NTOK, -1),
                  w_gu[l].astype(bf16), b_gu[l], w_dn[l].astype(bf16), b_dn[l], xa, mod_all, S)

    return _final_norm(xa, g_final, S)
```

```python
import functools

import jax
import jax.numpy as jnp
import numpy as np
from jax import lax
from jax.experimental import pallas as pl
from jax.experimental.pallas import tpu as pltpu

GRID_W = 64
HEAD_DIM = 64
EPS = 1e-6

A_HEADS = 8
A_KV_HEADS = 2
A_GROUP = A_HEADS // A_KV_HEADS
A_BLOCK = 128
ROPE_THETA = 10000.0

LRU_WIDTH = 512
LRU_BLOCKS = 8
LRU_BLOCK_W = LRU_WIDTH // LRU_BLOCKS
LRU_C = 8.0

CONV_W = 4
CONV_PAD = ((CONV_W - 1) // 2, CONV_W // 2)

DN_HEADS = 4
DN_DK = 128
DN_DV = 128
DN_CHUNK = 64

NA_HEADS = 8
NA_WIN_R = 8
NA_WIN_C = 16

N_BRANCH = 4
BRANCH_W = 512

TOP_K = 4
SWIGLU_LIMIT = 7.0
SWIGLU_ALPHA = 1.702

V7X_LANES = 128
V7X_MXU_DIM = 256
V7X_VMEM_BYTES = 64 * 1024 * 1024

TOKEN_TILE = 256
MOE_TILE = 256


def _in_splits(d_model):
    return (
        ('a_q', A_HEADS * HEAD_DIM), ('a_k', A_KV_HEADS * HEAD_DIM), ('a_v', A_KV_HEADS * HEAD_DIM),
        ('b_x', LRU_WIDTH), ('b_y', LRU_WIDTH),
        ('c_q', DN_HEADS * DN_DK), ('c_k', DN_HEADS * DN_DK), ('c_v', DN_HEADS * DN_DV),
        ('c_z', DN_HEADS * DN_DV), ('c_a', 2 * DN_HEADS), ('c_b', 2 * DN_HEADS),
        ('d_q', NA_HEADS * HEAD_DIM), ('d_k', NA_HEADS * HEAD_DIM), ('d_v', NA_HEADS * HEAD_DIM),
        ('gate', N_BRANCH * d_model),
    )


def _vmem_params(nbytes, semantics):
    limit = int(min(max(2 * nbytes, 32 * 1024 * 1024), V7X_VMEM_BYTES - 8 * 1024 * 1024))
    return pltpu.CompilerParams(dimension_semantics=semantics, vmem_limit_bytes=limit)


def _norm_mod_kernel(x_ref, g_ref, mod_ref, h_ref, *, shift_idx, scale_idx):
    x = x_ref[0]
    y = x * lax.rsqrt(jnp.mean(x * x, axis=-1, keepdims=True) + EPS) * g_ref[...]
    shift = mod_ref[0, 0, shift_idx:shift_idx + 1, :]
    scale = mod_ref[0, 0, scale_idx:scale_idx + 1, :]
    h_ref[0] = (y * (1 + scale) + shift).astype(h_ref.dtype)


def _norm_mod_router_kernel(x_ref, g_ref, mod_ref, wr_ref, br_ref, h_ref, lg_ref, *, shift_idx, scale_idx):
    x = x_ref[0]
    y = x * lax.rsqrt(jnp.mean(x * x, axis=-1, keepdims=True) + EPS) * g_ref[...]
    shift = mod_ref[0, 0, shift_idx:shift_idx + 1, :]
    scale = mod_ref[0, 0, scale_idx:scale_idx + 1, :]
    h = y * (1 + scale) + shift
    h_ref[0] = h.astype(h_ref.dtype)
    lg_ref[0] = jnp.dot(h, wr_ref[...], preferred_element_type=jnp.float32,
                        precision=lax.Precision.HIGHEST) + br_ref[...]


def _norm_mod(xa, g, mod_all, n_lat_tiles, shift_idx, scale_idx, router=None):
    B, NTOK, D = xa.shape
    tm = TOKEN_TILE
    grid = (B, NTOK // tm)
    x_spec = pl.BlockSpec((1, tm, D), lambda b, t: (b, t, 0))
    g_spec = pl.BlockSpec((1, D), lambda b, t: (0, 0))
    mod_spec = pl.BlockSpec((1, 1, 6, D), lambda b, t: (b, jnp.where(t >= n_lat_tiles, 1, 0), 0, 0))
    h_spec = pl.BlockSpec((1, tm, D), lambda b, t: (b, t, 0))
    params = _vmem_params(8 * tm * D * 4, ("parallel", "parallel"))
    if router is None:
        return pl.pallas_call(
            functools.partial(_norm_mod_kernel, shift_idx=shift_idx, scale_idx=scale_idx),
            grid=grid, in_specs=[x_spec, g_spec, mod_spec], out_specs=h_spec,
            out_shape=jax.ShapeDtypeStruct((B, NTOK, D), jnp.bfloat16),
            compiler_params=params, name="norm_mod")(xa, g.reshape(1, D), mod_all)
    w_router, b_router = router
    E = w_router.shape[1]
    return pl.pallas_call(
        functools.partial(_norm_mod_router_kernel, shift_idx=shift_idx, scale_idx=scale_idx),
        grid=grid,
        in_specs=[x_spec, g_spec, mod_spec, pl.BlockSpec((D, E), lambda b, t: (0, 0)),
                  pl.BlockSpec((1, E), lambda b, t: (0, 0))],
        out_specs=[h_spec, pl.BlockSpec((1, tm, E), lambda b, t: (b, t, 0))],
        out_shape=[jax.ShapeDtypeStruct((B, NTOK, D), jnp.float32),
                   jax.ShapeDtypeStruct((B, NTOK, E), jnp.float32)],
        compiler_params=params, name="norm_mod_router")(xa, g.reshape(1, D), mod_all, w_router,
                                                         b_router.reshape(1, E))


def _mm_kernel(a_ref, w_ref, o_ref):
    o_ref[...] = jnp.dot(a_ref[...], w_ref[...], preferred_element_type=jnp.float32).astype(o_ref.dtype)


def _matmul(a, w, out_dtype, tm=512, tn=512):
    M, K = a.shape
    N = w.shape[1]
    assert M % tm == 0 and N % tn == 0, (M, N, tm, tn)
    nbytes = 2 * (tm * K * a.dtype.itemsize + K * tn * w.dtype.itemsize + tm * tn * 4)
    return pl.pallas_call(
        _mm_kernel, grid=(N // tn, M // tm),
        in_specs=[pl.BlockSpec((tm, K), lambda j, i: (i, 0)), pl.BlockSpec((K, tn), lambda j, i: (0, j))],
        out_specs=pl.BlockSpec((tm, tn), lambda j, i: (i, j)),
        out_shape=jax.ShapeDtypeStruct((M, N), out_dtype),
        compiler_params=_vmem_params(nbytes, ("parallel", "parallel")), name="matmul")(a, w)


def _merge_kernel(ya_ref, yb_ref, yc_ref, yd_ref, gl_ref, wb_ref, wo_ref, x_ref, mod_ref, o_ref, *, d_model):
    m = None
    for i, y_ref in enumerate((ya_ref, yb_ref, yc_ref, yd_ref)):
        gates = jax.nn.sigmoid(gl_ref[0, :, i * d_model:(i + 1) * d_model].astype(jnp.float32))
        p = gates * jnp.dot(y_ref[0], wb_ref[i], preferred_element_type=jnp.float32)
        m = p if m is None else m + p
    out = jnp.dot(m.astype(jnp.bfloat16), wo_ref[...], preferred_element_type=jnp.float32)
    o_ref[0] = x_ref[0] + mod_ref[0, 0, 2:3, :] * out


def _merge(ys, gate_logits, w_branch, w_out, xa, mod_all, n_lat_tiles):
    B, NTOK, D = xa.shape
    tm = TOKEN_TILE
    tok = lambda w: pl.BlockSpec((1, tm, w), lambda b, t: (b, t, 0))
    nbytes = (2 * (4 * tm * BRANCH_W * 2 + tm * 4 * D * 4 + 2 * tm * D * 4)
              + 2 * (N_BRANCH * BRANCH_W * D * 2 + D * D * 2) + 4 * tm * D * 4)
    return pl.pallas_call(
        functools.partial(_merge_kernel, d_model=D), grid=(B, NTOK // tm),
        in_specs=[tok(BRANCH_W)] * 4 + [
            tok(N_BRANCH * D),
            pl.BlockSpec((N_BRANCH, BRANCH_W, D), lambda b, t: (0, 0, 0)),
            pl.BlockSpec((D, D), lambda b, t: (0, 0)),
            tok(D),
            pl.BlockSpec((1, 1, 6, D), lambda b, t: (b, jnp.where(t >= n_lat_tiles, 1, 0), 0, 0))],
        out_specs=tok(D), out_shape=jax.ShapeDtypeStruct((B, NTOK, D), jnp.float32),
        compiler_params=_vmem_params(nbytes, ("parallel", "parallel")), name="merge")(
            *ys, gate_logits, w_branch, w_out, xa, mod_all)


def _moe_kernel(te_ref, nv_ref, x_ref, wgu_ref, bgu_ref, wdn_ref, bdn_ref, o_ref, *, d_expert):
    i = pl.program_id(0)

    @pl.when(i < nv_ref[0])
    def _():
        gu = jnp.dot(x_ref[...].astype(jnp.bfloat16), wgu_ref[0], preferred_element_type=jnp.float32) + bgu_ref[0]
        gate = jnp.minimum(gu[:, :d_expert], SWIGLU_LIMIT)
        up = jnp.clip(gu[:, d_expert:], -SWIGLU_LIMIT, SWIGLU_LIMIT)
        act = (up + 1) * gate * jax.nn.sigmoid(SWIGLU_ALPHA * gate)
        y = jnp.dot(act.astype(jnp.bfloat16), wdn_ref[0], preferred_element_type=jnp.float32) + bdn_ref[0]
        o_ref[...] = y.astype(o_ref.dtype)

    @pl.when(i >= nv_ref[0])
    def _():
        o_ref[...] = jnp.zeros_like(o_ref)


def _moe_experts(x_sorted, tile_expert, n_valid, w_gu, b_gu, w_dn, b_dn):
    P, D = x_sorted.shape
    E, _, DE2 = w_gu.shape
    DE = DE2 // 2
    tm = MOE_TILE
    grid_spec = pltpu.PrefetchScalarGridSpec(
        num_scalar_prefetch=2, grid=(P // tm,),
        in_specs=[pl.BlockSpec((tm, D), lambda i, te, nv: (i, 0)),
                  pl.BlockSpec((1, D, DE2), lambda i, te, nv: (te[i], 0, 0)),
                  pl.BlockSpec((1, 1, DE2), lambda i, te, nv: (te[i], 0, 0)),
                  pl.BlockSpec((1, DE, D), lambda i, te, nv: (te[i], 0, 0)),
                  pl.BlockSpec((1, 1, D), lambda i, te, nv: (te[i], 0, 0))],
        out_specs=pl.BlockSpec((tm, D), lambda i, te, nv: (i, 0)))
    nbytes = 2 * (tm * D * 4 + D * DE2 * 2 + DE * D * 2 + tm * D * 4) + tm * DE2 * 4 * 3
    return pl.pallas_call(
        functools.partial(_moe_kernel, d_expert=DE), grid_spec=grid_spec,
        out_shape=jax.ShapeDtypeStruct((P, D), jnp.float32),
        compiler_params=_vmem_params(nbytes, ("arbitrary",)), name="moe_experts")(
            tile_expert, n_valid, x_sorted, w_gu, b_gu.reshape(E, 1, DE2), w_dn, b_dn.reshape(E, 1, D))


_ROUTE_TILE = 256


def _moe_route_kernel(e_ref, base_ref, dest_ref, carry_ref):
    n_exp, n = base_ref.shape[0], e_ref.shape[2]

    @pl.when(pl.program_id(0) == 0)
    def _():
        carry_ref[...] = jnp.zeros_like(carry_ref)

    hot = jnp.where(lax.broadcasted_iota(jnp.int32, (n_exp, n), 0) == e_ref[0], 1.0, 0.0)
    earlier = jnp.where(lax.broadcasted_iota(jnp.int32, (n, n), 0) < lax.broadcasted_iota(jnp.int32, (n, n), 1),
                        1.0, 0.0)
    rank = jnp.dot(hot.astype(jnp.bfloat16), earlier.astype(jnp.bfloat16), preferred_element_type=jnp.float32)
    slot = jnp.sum(hot * (rank + carry_ref[...] + base_ref[...]), axis=0, keepdims=True)
    dest_ref[0] = slot.astype(jnp.int32)
    carry_ref[...] = carry_ref[...] + jnp.sum(hot, axis=1, keepdims=True)


def _moe_route(flat_e, pad_start):
    n_pairs = flat_e.shape[0]
    n_exp = pad_start.shape[0]
    n = _ROUTE_TILE
    assert n_pairs % n == 0 and n_pairs + n_exp * MOE_TILE < 2 ** 24
    dest = pl.pallas_call(
        _moe_route_kernel, grid=(n_pairs // n,),
        in_specs=[pl.BlockSpec((1, 1, n), lambda i: (i, 0, 0)), pl.BlockSpec((n_exp, 1), lambda i: (0, 0))],
        out_specs=pl.BlockSpec((1, 1, n), lambda i: (i, 0, 0)),
        out_shape=jax.ShapeDtypeStruct((n_pairs // n, 1, n), jnp.int32),
        scratch_shapes=[pltpu.VMEM((n_exp, 1), jnp.float32)],
        compiler_params=_vmem_params(16 * n * n * 4, ("arbitrary",)), name="moe_route")(
            flat_e.reshape(n_pairs // n, 1, n), pad_start.astype(jnp.float32).reshape(n_exp, 1))
    return dest.reshape(n_pairs)


_DISPATCH_ROWS = 256
_COMBINE_ROWS = 128


def _row_copy(src_ref, src_row, dst_ref, dst_row, sem):
    return pltpu.make_async_copy(src_ref.at[pl.ds(src_row, 1)], dst_ref.at[pl.ds(dst_row, 1)], sem)


def _moe_dispatch_kernel(dest_ref, h_ref, xs_in_ref, xs_ref, sem):
    del xs_in_ref
    rows = _DISPATCH_ROWS
    base = pl.program_id(0) * rows * TOP_K

    def issue(r, carry):
        for k in range(TOP_K):
            _row_copy(h_ref, r, xs_ref, dest_ref[base + r * TOP_K + k], sem).start()
        return carry

    def drain(r, carry):
        for _ in range(TOP_K):
            _row_copy(h_ref, 0, xs_ref, 0, sem).wait()
        return carry

    lax.fori_loop(0, rows, issue, 0, unroll=8)
    lax.fori_loop(0, rows, drain, 0)


def _moe_dispatch(h2, dest, n_slots):
    T, D = h2.shape
    rows = _DISPATCH_ROWS
    assert T % rows == 0
    grid_spec = pltpu.PrefetchScalarGridSpec(
        num_scalar_prefetch=1, grid=(T // rows,),
        in_specs=[pl.BlockSpec((rows, D), lambda i, d: (i, 0)), pl.BlockSpec(memory_space=pl.ANY)],
        out_specs=pl.BlockSpec(memory_space=pl.ANY),
        scratch_shapes=[pltpu.SemaphoreType.DMA(())])
    return pl.pallas_call(
        _moe_dispatch_kernel, grid_spec=grid_spec, out_shape=jax.ShapeDtypeStruct((n_slots, D), h2.dtype),
        input_output_aliases={2: 0},
        compiler_params=_vmem_params(4 * rows * D * 4, ("arbitrary",)), name="moe_dispatch")(
            dest, h2, jnp.zeros((n_slots, D), h2.dtype))


def _moe_combine_kernel(dest_ref, y_ref, w_ref, x_ref, mod_ref, o_ref, buf, sems):
    rows = _COMBINE_ROWS
    lin = pl.program_id(0) * pl.num_programs(1) + pl.program_id(1)
    n_steps = pl.num_programs(0) * pl.num_programs(1)

    def issue(step, slot):
        def body(r, carry):
            t = step * rows + r
            for k in range(TOP_K):
                pltpu.make_async_copy(y_ref.at[pl.ds(dest_ref[t * TOP_K + k], 1)], buf.at[slot, k, pl.ds(r, 1)],
                                      sems.at[slot]).start()
            return carry
        lax.fori_loop(0, rows, body, 0, unroll=8)

    @pl.when(lin == 0)
    def _():
        issue(0, 0)

    @pl.when(lin + 1 < n_steps)
    def _():
        issue(lin + 1, (lin + 1) % 2)

    slot = lin % 2

    def wait_row(r, carry):
        for k in range(TOP_K):
            pltpu.make_async_copy(y_ref.at[pl.ds(0, 1)], buf.at[slot, k, pl.ds(0, 1)], sems.at[slot]).wait()
        return carry
    lax.fori_loop(0, rows, wait_row, 0)

    f = None
    for k in range(TOP_K):
        term = w_ref[:, k:k + 1] * buf[slot, k]
        f = term if f is None else f + term
    o_ref[0] = x_ref[0] + mod_ref[0, 0, 5:6, :] * f


def _moe_combine(y_sorted, dest, top_w, xa, mod_all, seq):
    B, NTOK, D = xa.shape
    rows = _COMBINE_ROWS
    assert seq % rows == 0 and NTOK % rows == 0
    n_lat = seq // rows
    tpb = NTOK // rows
    grid_spec = pltpu.PrefetchScalarGridSpec(
        num_scalar_prefetch=1, grid=(B, tpb),
        in_specs=[pl.BlockSpec(memory_space=pl.ANY),
                  pl.BlockSpec((rows, TOP_K), lambda b, t, d: (b * tpb + t, 0)),
                  pl.BlockSpec((1, rows, D), lambda b, t, d: (b, t, 0)),
                  pl.BlockSpec((1, 1, 6, D), lambda b, t, d: (b, jnp.where(t >= n_lat, 1, 0), 0, 0))],
        out_specs=pl.BlockSpec((1, rows, D), lambda b, t, d: (b, t, 0)),
        scratch_shapes=[pltpu.VMEM((2, TOP_K, rows, D), jnp.float32), pltpu.SemaphoreType.DMA((2,))])
    nbytes = 2 * TOP_K * rows * D * 4 + 8 * rows * D * 4
    return pl.pallas_call(
        _moe_combine_kernel, grid_spec=grid_spec, out_shape=jax.ShapeDtypeStruct((B, NTOK, D), jnp.float32),
        compiler_params=_vmem_params(nbytes, ("arbitrary", "arbitrary")), name="moe_combine")(
            dest, y_sorted, top_w, xa, mod_all)


def _moe(h2, logits, w_gu, b_gu, w_dn, b_dn, xa, mod_all, seq):
    T, D = h2.shape
    E = logits.shape[1]
    tm = MOE_TILE
    top_v, top_i = lax.top_k(logits, TOP_K)
    top_w = jax.nn.softmax(top_v, axis=-1)
    flat_e = top_i.reshape(-1).astype(jnp.int32)
    n_pairs = T * TOP_K
    n_slots = n_pairs + E * tm
    counts = jnp.sum(jax.nn.one_hot(flat_e, E, dtype=jnp.int32), axis=0)
    padded = ((counts + tm - 1) // tm) * tm
    pad_end = jnp.cumsum(padded)
    n_valid = (pad_end[-1] // tm).astype(jnp.int32).reshape(1)
    tile_start = jnp.arange(n_slots // tm, dtype=jnp.int32) * tm
    tile_expert = jnp.minimum(jnp.searchsorted(pad_end, tile_start, side='right'), E - 1).astype(jnp.int32)
    last_expert = tile_expert[jnp.maximum(n_valid[0] - 1, 0)]
    tile_expert = jnp.where(tile_start < pad_end[-1], tile_expert, last_expert)
    dest = _moe_route(flat_e, pad_end - padded)
    x_sorted = _moe_dispatch(h2, dest, n_slots)
    y_sorted = _moe_experts(x_sorted, tile_expert, n_valid, w_gu, b_gu, w_dn, b_dn)
    return _moe_combine(y_sorted, dest, top_w, xa, mod_all, seq)


def _final_norm_kernel(x_ref, g_ref, o_ref):
    x = x_ref[0]
    o_ref[0] = x * lax.rsqrt(jnp.mean(x * x, axis=-1, keepdims=True) + EPS) * g_ref[...]


def _final_norm(xa, g, seq):
    B, _, D = xa.shape
    tm = TOKEN_TILE
    return pl.pallas_call(
        _final_norm_kernel, grid=(B, seq // tm),
        in_specs=[pl.BlockSpec((1, tm, D), lambda b, t: (b, t, 0)), pl.BlockSpec((1, D), lambda b, t: (0, 0))],
        out_specs=pl.BlockSpec((1, tm, D), lambda b, t: (b, t, 0)),
        out_shape=jax.ShapeDtypeStruct((B, seq, D), jnp.float32),
        compiler_params=_vmem_params(4 * tm * D * 4, ("parallel", "parallel")), name="final_norm")(
            xa, g.reshape(1, D))


def _seg_mean_matrix(width, seg):
    idx = np.arange(width) // seg
    return jnp.asarray((idx[:, None] == idx[None, :]).astype(np.float32) / seg, jnp.bfloat16)


def _seg_mean_sq(x, m_ref):
    sq = x * x
    hi = sq.astype(jnp.bfloat16)
    lo = (sq - hi.astype(jnp.float32)).astype(jnp.bfloat16)
    return (jnp.dot(hi, m_ref[...], preferred_element_type=jnp.float32)
            + jnp.dot(lo, m_ref[...], preferred_element_type=jnp.float32))


def _rope(x, cos, sin_signed):
    width = x.shape[1]
    half = HEAD_DIM // 2
    lane = lax.broadcasted_iota(jnp.int32, x.shape, 1)
    fwd = pltpu.roll(x, width - half, axis=1)
    bwd = pltpu.roll(x, half, axis=1)
    rot = jnp.where((lane & (HEAD_DIM - 1)) < half, fwd, bwd)
    return x * cos + rot * sin_signed


def _rope_tables(seq, n_ctx):
    t = jnp.arange(seq)
    row = (t // GRID_W).astype(jnp.float32)
    col = (t % GRID_W).astype(jnp.float32)
    n_freq = HEAD_DIM // 4
    inv = ROPE_THETA ** (-jnp.arange(n_freq, dtype=jnp.float32) / n_freq)
    ang = jnp.concatenate([row[:, None] * inv[None], col[:, None] * inv[None]], axis=-1)
    cos, sin = jnp.cos(ang), jnp.sin(ang)
    reps = V7X_LANES // HEAD_DIM
    cos_t = jnp.tile(jnp.concatenate([cos, cos], axis=-1), (1, reps))
    sin_t = jnp.tile(jnp.concatenate([-sin, sin], axis=-1), (1, reps))
    cos_t = jnp.concatenate([cos_t, jnp.ones((n_ctx, V7X_LANES), jnp.float32)], axis=0)
    sin_t = jnp.concatenate([sin_t, jnp.zeros((n_ctx, V7X_LANES), jnp.float32)], axis=0)
    return cos_t, sin_t


_A_QW = A_HEADS * HEAD_DIM
_A_KW = A_KV_HEADS * HEAD_DIM


def _gqa_prep_kernel(u_ref, cos_ref, sin_ref, gq_ref, gk_ref, mq_ref, mk_ref, q_ref, k_ref):
    u = u_ref[0].astype(jnp.float32)
    q = u[:, :_A_QW]
    k = u[:, _A_QW:_A_QW + _A_KW]
    qn = q * lax.rsqrt(_seg_mean_sq(q, mq_ref) + EPS) * gq_ref[...]
    kn = k * lax.rsqrt(_seg_mean_sq(k, mk_ref) + EPS) * gk_ref[...]
    cos = cos_ref[...]
    sin = sin_ref[...]
    reps = _A_QW // V7X_LANES
    qr = _rope(qn, jnp.concatenate([cos] * reps, axis=1), jnp.concatenate([sin] * reps, axis=1))
    kr = _rope(kn, cos, sin)
    q_ref[0] = (qr * (HEAD_DIM ** -0.5)).astype(q_ref.dtype)
    k_ref[0] = kr.astype(k_ref.dtype)


def _gqa_prep(ua, cos_t, sin_t, g_q, g_k):
    B, NTOK, W = ua.shape
    tm = TOKEN_TILE
    gq = jnp.tile(g_q, A_HEADS).reshape(1, _A_QW)
    gk = jnp.tile(g_k, A_KV_HEADS).reshape(1, _A_KW)
    mq = _seg_mean_matrix(_A_QW, HEAD_DIM)
    mk = _seg_mean_matrix(_A_KW, HEAD_DIM)
    const = lambda shape: pl.BlockSpec(shape, lambda b, t: (0,) * len(shape))
    return pl.pallas_call(
        _gqa_prep_kernel, grid=(B, NTOK // tm),
        in_specs=[pl.BlockSpec((1, tm, W), lambda b, t: (b, t, 0)),
                  pl.BlockSpec((tm, V7X_LANES), lambda b, t: (t, 0)),
                  pl.BlockSpec((tm, V7X_LANES), lambda b, t: (t, 0)),
                  const((1, _A_QW)), const((1, _A_KW)), const((_A_QW, _A_QW)), const((_A_KW, _A_KW))],
        out_specs=[pl.BlockSpec((1, tm, _A_QW), lambda b, t: (b, t, 0)),
                   pl.BlockSpec((1, tm, _A_KW), lambda b, t: (b, t, 0))],
        out_shape=[jax.ShapeDtypeStruct((B, NTOK, _A_QW), jnp.bfloat16),
                   jax.ShapeDtypeStruct((B, NTOK, _A_KW), jnp.bfloat16)],
        compiler_params=_vmem_params(16 * tm * W * 4, ("parallel", "parallel")), name="gqa_prep")(
            ua, cos_t, sin_t, gq, gk, mq, mk)


def _dot_nt(a, b):
    return lax.dot_general(a, b, (((1,), (1,)), ((), ())), preferred_element_type=jnp.float32)


def _softmax_pv(scores, values):
    m = None
    for s in scores:
        mi = jnp.max(s, axis=-1, keepdims=True)
        m = mi if m is None else jnp.maximum(m, mi)
    acc, den = None, None
    for s, v in zip(scores, values):
        p = jnp.exp(s - m)
        li = jnp.sum(p, axis=-1, keepdims=True)
        oi = jnp.dot(p.astype(jnp.bfloat16), v, preferred_element_type=jnp.float32)
        acc = oi if acc is None else acc + oi
        den = li if den is None else den + li
    return acc / den


def _gqa_attn_kernel(q_ref, k_ref, v_ref, o_ref, *, seq, n_lat_tiles):
    t = pl.program_id(1)
    ntok = k_ref.shape[1]

    def attend(lo):
        outs = []
        for j in range(A_KV_HEADS):
            kv = slice(j * HEAD_DIM, (j + 1) * HEAD_DIM)
            kj = k_ref[0, lo:ntok, kv]
            vj = v_ref[0, lo:ntok, kv]
            for g in range(A_GROUP):
                h = j * A_GROUP + g
                qh = q_ref[0, :, h * HEAD_DIM:(h + 1) * HEAD_DIM]
                outs.append(_softmax_pv([_dot_nt(qh, kj)], [vj]))
        o_ref[0] = jnp.concatenate(outs, axis=-1).astype(o_ref.dtype)

    @pl.when(t < n_lat_tiles)
    def _():
        attend(0)

    @pl.when(t >= n_lat_tiles)
    def _():
        attend(seq)


def _gqa_attn(qn, kn, ua, seq):
    B, NTOK, _ = qn.shape
    tq = TOKEN_TILE
    v_block = (_A_QW + _A_KW) // _A_KW
    nbytes = 4 * NTOK * _A_KW * 2 + 4 * tq * _A_QW * 2 + 6 * tq * NTOK * 4
    return pl.pallas_call(
        functools.partial(_gqa_attn_kernel, seq=seq, n_lat_tiles=seq // tq), grid=(B, NTOK // tq),
        in_specs=[pl.BlockSpec((1, tq, _A_QW), lambda b, t: (b, t, 0)),
                  pl.BlockSpec((1, NTOK, _A_KW), lambda b, t: (b, 0, 0)),
                  pl.BlockSpec((1, NTOK, _A_KW), lambda b, t: (b, 0, v_block))],
        out_specs=pl.BlockSpec((1, tq, _A_QW), lambda b, t: (b, t, 0)),
        out_shape=jax.ShapeDtypeStruct((B, NTOK, _A_QW), jnp.bfloat16),
        compiler_params=_vmem_params(nbytes, ("parallel", "arbitrary")), name="gqa_attn")(qn, kn, ua)


_NA_W = NA_HEADS * HEAD_DIM
_NA_KBLOCK = 3 * TOKEN_TILE
_NA_MASKED = -1e30


def _na_bias_table(rpb, seq):
    rows = seq // GRID_W
    rows_per_tile = TOKEN_TILE // GRID_W
    n_tiles = rows // rows_per_tile
    assert TOKEN_TILE % GRID_W == 0 and rows % rows_per_tile == 0 and n_tiles >= 3 and rows >= NA_WIN_R
    assert rows_per_tile + NA_WIN_R <= _NA_KBLOCK // GRID_W
    key_rows = _NA_KBLOCK // GRID_W
    n_r, n_c = 2 * NA_WIN_R - 1, 2 * NA_WIN_C - 1
    col = np.arange(GRID_W)
    cidx = np.clip(col[None, :] - col[:, None] + NA_WIN_C - 1, 0, n_c - 1)
    c_hot = (np.arange(n_c)[:, None, None] == cidx[None]).astype(np.float32)
    cs = np.clip(col - NA_WIN_C // 2, 0, GRID_W - NA_WIN_C)
    c_ok = (col[None, :] >= cs[:, None]) & (col[None, :] < cs[:, None] + NA_WIN_C)
    tables = []
    for t in (0, 1, n_tiles - 1):
        qr = t * rows_per_tile + np.arange(rows_per_tile)
        kr = int(np.clip(t - 1, 0, n_tiles - 3)) * rows_per_tile + np.arange(key_rows)
        rs = np.clip(qr - NA_WIN_R // 2, 0, rows - NA_WIN_R)
        r_ok = (kr[None, :] >= rs[:, None]) & (kr[None, :] < rs[:, None] + NA_WIN_R)
        ridx = np.clip(kr[None, :] - qr[:, None] + NA_WIN_R - 1, 0, n_r - 1)
        r_hot = (np.arange(n_r)[:, None, None] == ridx[None]).astype(np.float32)
        b = jnp.einsum('rak,hrc,cqj->haqkj', r_hot, rpb, c_hot, precision=lax.Precision.HIGHEST)
        valid = r_ok[:, None, :, None] & c_ok[None, :, None, :]
        tables.append(jnp.where(valid[None], b, _NA_MASKED).reshape(NA_HEADS, TOKEN_TILE, _NA_KBLOCK))
    return jnp.stack(tables)


def _na_attn_kernel(q_ref, k_ref, v_ref, bias_ref, o_ref, *, seq, n_lat_tiles):
    t = pl.program_id(1)
    ntok = k_ref.shape[1]
    scale = HEAD_DIM ** -0.5

    @pl.when(t < n_lat_tiles)
    def _():
        k_off = pl.multiple_of(jnp.clip(t - 1, 0, n_lat_tiles - 3) * TOKEN_TILE, TOKEN_TILE)
        outs = []
        for h in range(NA_HEADS):
            hs = slice(h * HEAD_DIM, (h + 1) * HEAD_DIM)
            qh = q_ref[0, :, hs] * scale
            s_nb = _dot_nt(qh, k_ref[0, pl.ds(k_off, _NA_KBLOCK), hs]) + bias_ref[0, h]
            s_ctx = _dot_nt(qh, k_ref[0, seq:ntok, hs])
            outs.append(_softmax_pv([s_nb, s_ctx], [v_ref[0, pl.ds(k_off, _NA_KBLOCK), hs], v_ref[0, seq:ntok, hs]]))
        o_ref[0] = jnp.concatenate(outs, axis=-1).astype(o_ref.dtype)

    @pl.when(t >= n_lat_tiles)
    def _():
        outs = []
        for h in range(NA_HEADS):
            hs = slice(h * HEAD_DIM, (h + 1) * HEAD_DIM)
            qh = q_ref[0, :, hs] * scale
            outs.append(_softmax_pv([_dot_nt(qh, k_ref[0, seq:ntok, hs])], [v_ref[0, seq:ntok, hs]]))
        o_ref[0] = jnp.concatenate(outs, axis=-1).astype(o_ref.dtype)


def _na_attn(ud, bias, seq):
    B, NTOK, _ = ud.shape
    tq = TOKEN_TILE
    n_lat_tiles = seq // tq
    nbytes = (4 * NTOK * _NA_W * 2 + 4 * tq * _NA_W * 2 + 2 * NA_HEADS * tq * _NA_KBLOCK * 4
              + 8 * tq * (_NA_KBLOCK + NTOK - seq) * 4)
    return pl.pallas_call(
        functools.partial(_na_attn_kernel, seq=seq, n_lat_tiles=n_lat_tiles), grid=(B, NTOK // tq),
        in_specs=[pl.BlockSpec((1, tq, _NA_W), lambda b, t: (b, t, 0)),
                  pl.BlockSpec((1, NTOK, _NA_W), lambda b, t: (b, 0, 1)),
                  pl.BlockSpec((1, NTOK, _NA_W), lambda b, t: (b, 0, 2)),
                  pl.BlockSpec((1, NA_HEADS, tq, _NA_KBLOCK),
                               lambda b, t: (jnp.where(t == 0, 0, jnp.where(t >= n_lat_tiles - 1, 2, 1)), 0, 0, 0))],
        out_specs=pl.BlockSpec((1, tq, _NA_W), lambda b, t: (b, t, 0)),
        out_shape=jax.ShapeDtypeStruct((B, NTOK, _NA_W), jnp.bfloat16),
        compiler_params=_vmem_params(nbytes, ("parallel", "arbitrary")), name="na_attn")(ud, ud, ud, bias)


_HALO = 8


def _dwconv_tile(prev8, cur, next8, w, row0, seq, ntok):
    x = jnp.concatenate([prev8, cur, next8], axis=0)
    n = x.shape[0]
    t = row0 - _HALO + lax.broadcasted_iota(jnp.int32, (n, 1), 0)
    keep = lambda hit: jnp.where(hit, 0.0, 1.0)
    first = keep(t == 0) * keep(t == seq)
    last1 = keep(t == seq - 1) * keep(t == ntok - 1)
    last2 = last1 * keep(t == seq - 2) * keep(t == ntok - 2)
    xm1 = pltpu.roll(x, 1, axis=0) * first
    xp1 = pltpu.roll(x, n - 1, axis=0) * last1
    xp2 = pltpu.roll(x, n - 2, axis=0) * last2
    y = w[0:1] * xm1 + w[1:2] * x + w[2:3] * xp1 + w[3:4] * xp2
    return y[_HALO:n - _HALO]


def _softplus(z):
    return jnp.log1p(jnp.exp(-jnp.abs(z))) + jnp.maximum(z, 0.0)


_LRU_CB = V7X_LANES


def _scan_tile(a, b, h_prev, reverse):
    n = a.shape[0]
    row = lax.broadcasted_iota(jnp.int32, (n, 1), 0)
    d = 1
    while d < n:
        if reverse:
            inside = row < n - d
            a_sh = pltpu.roll(a, n - d, axis=0)
            b_sh = pltpu.roll(b, n - d, axis=0)
        else:
            inside = row >= d
            a_sh = pltpu.roll(a, d, axis=0)
            b_sh = pltpu.roll(b, d, axis=0)
        b = jnp.where(inside, a * b_sh, 0.0) + b
        a = jnp.where(inside, a * a_sh, a)
        d *= 2
    h = a * h_prev + b
    return h, (h[0:1] if reverse else h[n - 1:n])


def _rglru_kernel(x_ref, y_ref, cw_ref, cb_ref, lam_ref, wr_ref, br_ref, wi_ref, bi_ref, o_ref, hf_ref,
                  *, seq, ntok):
    tm = TOKEN_TILE
    n_tiles, n_lat = ntok // tm, seq // tm
    n_ctx = n_tiles - n_lat

    def coeffs(i, d):
        r0 = pl.multiple_of(i * tm, tm)
        p0 = pl.multiple_of(jnp.maximum(r0 - _HALO, 0), _HALO)
        n0 = pl.multiple_of(jnp.minimum(r0 + tm, ntok - _HALO), _HALO)
        xc = _dwconv_tile(x_ref[0, pl.ds(p0, _HALO), :], x_ref[0, pl.ds(r0, tm), :], x_ref[0, pl.ds(n0, _HALO), :],
                          cw_ref[...], r0, seq, ntok) + cb_ref[...]
        xb = xc.astype(jnp.bfloat16)
        r = jax.nn.sigmoid(jnp.dot(xb, wr_ref[d, 0], preferred_element_type=jnp.float32) + br_ref[d:d + 1])
        g = jax.nn.sigmoid(jnp.dot(xb, wi_ref[d, 0], preferred_element_type=jnp.float32) + bi_ref[d:d + 1])
        a = jnp.exp(-LRU_C * r * _softplus(-lam_ref[d:d + 1]))
        return r0, a, jnp.sqrt(1.0 - a * a) * (g * xc)

    def fwd(k, h):
        i = jnp.where(k < n_ctx, n_lat + k, k - n_ctx)
        r0, a, b = coeffs(i, 0)
        ht, h = _scan_tile(a, b, h, reverse=False)
        hf_ref[pl.ds(r0, tm), :] = ht
        return h

    def bwd(k, h):
        i = jnp.where(k < n_ctx, n_tiles - 1 - k, n_lat - 1 - (k - n_ctx))
        r0, a, b = coeffs(i, 1)
        ht, h = _scan_tile(a, b, h, reverse=True)
        yv = y_ref[0, pl.ds(r0, tm), :]
        o_ref[0, pl.ds(r0, tm), :] = ((hf_ref[pl.ds(r0, tm), :] + ht) * jax.nn.gelu(yv)).astype(o_ref.dtype)
        return h

    h0 = jnp.zeros((1, _LRU_CB), jnp.float32)
    lax.fori_loop(0, n_tiles, fwd, h0)
    lax.fori_loop(0, n_tiles, bwd, h0)


def _rglru(ub, seq, conv_w, conv_b, lam, w_r, b_r, w_i, b_i):
    B, NTOK, _ = ub.shape
    cb = _LRU_CB
    ncb = LRU_WIDTH // cb
    per = cb // LRU_BLOCK_W

    def block_diag(w):
        w = w.reshape(2, ncb, per, LRU_BLOCK_W, LRU_BLOCK_W)
        eye = jnp.eye(per, dtype=w.dtype)
        return jnp.einsum('dnpij,pq->dnpiqj', w, eye).reshape(2, ncb, cb, cb).astype(jnp.bfloat16)

    vec = lambda rows: pl.BlockSpec((rows, cb), lambda b, c: (0, c))
    wspec = pl.BlockSpec((2, 1, cb, cb), lambda b, c: (0, c, 0, 0))
    nbytes = 2 * (2 * NTOK * cb * 4 + NTOK * cb * 2) + NTOK * cb * 4 + 64 * TOKEN_TILE * cb * 4
    return pl.pallas_call(
        functools.partial(_rglru_kernel, seq=seq, ntok=NTOK), grid=(B, ncb),
        in_specs=[pl.BlockSpec((1, NTOK, cb), lambda b, c: (b, 0, c)),
                  pl.BlockSpec((1, NTOK, cb), lambda b, c: (b, 0, ncb + c)),
                  vec(CONV_W), vec(1), vec(2), wspec, vec(2), wspec, vec(2)],
        out_specs=pl.BlockSpec((1, NTOK, cb), lambda b, c: (b, 0, c)),
        out_shape=jax.ShapeDtypeStruct((B, NTOK, LRU_WIDTH), jnp.bfloat16),
        scratch_shapes=[pltpu.VMEM((NTOK, cb), jnp.float32)],
        compiler_params=_vmem_params(nbytes, ("parallel", "parallel")), name="rglru")(
            ub, ub, conv_w, conv_b.reshape(1, -1), lam, block_diag(w_r), b_r, block_diag(w_i), b_i)


_DN_W = DN_HEADS * DN_DK
_DN_NCH = TOKEN_TILE // DN_CHUNK
_DN_NDH = 2 * DN_HEADS


def _l2n(x):
    return x * lax.rsqrt(jnp.sum(x * x, axis=-1, keepdims=True) + EPS)


def _dot_f32(a, b):
    return jnp.dot(a, b, preferred_element_type=jnp.float32, precision=lax.Precision.HIGHEST)


def _bdot(a, b):
    return jnp.dot(a.astype(jnp.bfloat16), b.astype(jnp.bfloat16), preferred_element_type=jnp.float32)


def _diag_blocks(m):
    c = DN_CHUNK
    return jnp.concatenate([m[i * c:(i + 1) * c, i * c:(i + 1) * c] for i in range(_DN_NCH)], axis=0)


def _gdn_chunk_kernel(prev_ref, cur_ref, next_ref, ab_ref, abt_ref, cw_ref, al_ref, dt_ref, alt_ref, dtt_ref,
                      u_ref, w_ref, qg_ref, kdt_ref, qk_ref, gl_ref, *, seq, ntok):
    tm = TOKEN_TILE
    c = DN_CHUNK
    row0 = pl.program_id(1) * tm
    qkv = jax.nn.silu(_dwconv_tile(prev_ref[0], cur_ref[0], next_ref[0], cw_ref[...], row0, seq, ntok))

    ndh = _DN_NDH
    g_col = -jnp.exp(al_ref[...]) * _softplus(ab_ref[0, :, 0:ndh] + dt_ref[...])
    beta_col = jax.nn.sigmoid(ab_ref[0, :, ndh:2 * ndh])
    g_row = -jnp.exp(alt_ref[...]) * _softplus(abt_ref[0, 0:ndh, :] + dtt_ref[...])

    ri = lax.broadcasted_iota(jnp.int32, (tm, tm), 0)
    ci = lax.broadcasted_iota(jnp.int32, (tm, tm), 1)
    blk = lambda v, size: jnp.right_shift(v, size.bit_length() - 1)
    same = blk(ri, c) == blk(ci, c)
    one = lambda m: jnp.where(m, 1.0, 0.0)
    low = one(same) * one(ri >= ci)
    upp = one(same) * one(ri <= ci)
    low_s = one(same) * one(ri > ci)
    upp_s = one(same) * one(ri < ci)
    samef = one(same)
    eye = one(ri == ci)
    base = 8
    blk8 = one(blk(ri, base) == blk(ci, base))
    merge = ([], [])
    n = base
    while n < c:
        pair = one(blk(ri, 2 * n) == blk(ci, 2 * n))
        r_odd, c_odd = one((blk(ri, n) & 1) == 1), one((blk(ci, n) & 1) == 1)
        merge[0].append(pair * r_odd * (1.0 - c_odd))
        merge[1].append(pair * (1.0 - r_odd) * c_odd)
        n *= 2

    nh = DN_HEADS
    is_fwd_col = lax.broadcasted_iota(jnp.int32, (tm, ndh), 1) < nh
    is_fwd_row = lax.broadcasted_iota(jnp.int32, (ndh, tm), 0) < nh
    gc_col = jnp.where(is_fwd_col, _dot_f32(low, g_col), _dot_f32(upp, g_col))
    gc_row = jnp.where(is_fwd_row, _dot_f32(g_row, upp), _dot_f32(g_row, low))
    gsum_col = _dot_f32(samef, g_col)

    for h in range(nh):
        hs = slice(h * DN_DK, (h + 1) * DN_DK)
        qh = _l2n(qkv[:, h * DN_DK:(h + 1) * DN_DK]) * (DN_DK ** -0.5)
        kh = _l2n(qkv[:, _DN_W + h * DN_DK:_DN_W + (h + 1) * DN_DK])
        vh = qkv[:, 2 * _DN_W + h * DN_DV:2 * _DN_W + (h + 1) * DN_DV]
        kk = _dot_nt(kh.astype(jnp.bfloat16), kh.astype(jnp.bfloat16))
        qk = _dot_nt(qh.astype(jnp.bfloat16), kh.astype(jnp.bfloat16))
        for d in range(2):
            j = d * nh + h
            mask, mask_s = (low, low_s) if d == 0 else (upp, upp_s)
            gcc = gc_col[:, j:j + 1]
            diff = gcc - gc_row[j:j + 1, :]
            dec = mask * jnp.exp(mask * diff)
            beta = beta_col[:, j:j + 1]
            egc = jnp.exp(gcc)
            kb = kh * beta
            lm = mask_s * (beta * kk) * dec
            p = lm * blk8
            p2 = _bdot(p, p)
            tinv = eye - p
            tinv = tinv + _bdot(tinv, p2)
            tinv = tinv + _bdot(tinv, _bdot(p2, p2))
            for mm in merge[d]:
                tinv = tinv - _bdot(_bdot(tinv, lm * mm), tinv)
            x = _bdot(tinv, jnp.concatenate([vh * beta, kb * egc], axis=1))
            u_ref[0, d, :, hs] = x[:, :DN_DV]
            w_ref[0, d, :, hs] = x[:, DN_DV:].astype(w_ref.dtype)
            qg_ref[0, d, :, hs] = (qh * egc).astype(qg_ref.dtype)
            kd = kh * jnp.exp(gsum_col[:, j:j + 1] - gcc)
            kdt_ref[0, d, hs, :] = kd.T.astype(kdt_ref.dtype)
            qk_ref[0, d, :, h * c:(h + 1) * c] = _diag_blocks(qk * dec).astype(qk_ref.dtype)
            gl = jnp.exp(gsum_col[:, j:j + 1])
            for i in range(_DN_NCH):
                gl_ref[0, d, i, :, hs] = jnp.broadcast_to(gl[i * c:i * c + 8], (8, DN_DK))


def _gdn_chunk(uq, uab, uab_t, seq, conv_w, a_log, dt_bias):
    B, NTOK, W = uq.shape
    tm = TOKEN_TILE
    nt = NTOK // tm
    hb = tm // _HALO
    ndh = _DN_NDH
    al, dt = a_log.reshape(1, ndh), dt_bias.reshape(1, ndh)
    const = lambda shape: pl.BlockSpec(shape, lambda b, t: (0,) * len(shape))
    tile4 = lambda w: pl.BlockSpec((1, 2, tm, w), lambda b, t: (b, 0, t, 0))
    f32, bf16 = jnp.float32, jnp.bfloat16
    nbytes = 2 * (tm + 16) * W * 4 * 6 + 2 * 2 * tm * (_DN_W * 12 + 256 * 2) + 40 * tm * tm * 4
    return pl.pallas_call(
        functools.partial(_gdn_chunk_kernel, seq=seq, ntok=NTOK), grid=(B, nt),
        in_specs=[pl.BlockSpec((1, _HALO, W), lambda b, t: (b, jnp.maximum(t * hb - 1, 0), 0)),
                  pl.BlockSpec((1, tm, W), lambda b, t: (b, t, 0)),
                  pl.BlockSpec((1, _HALO, W), lambda b, t: (b, jnp.minimum((t + 1) * hb, nt * hb - 1), 0)),
                  pl.BlockSpec((1, tm, V7X_LANES), lambda b, t: (b, t, 0)),
                  pl.BlockSpec((1, V7X_LANES, tm), lambda b, t: (b, 0, t)),
                  const((CONV_W, W)), const((1, ndh)), const((1, ndh)), const((ndh, 1)), const((ndh, 1))],
        out_specs=[tile4(_DN_W), tile4(_DN_W), tile4(_DN_W),
                   pl.BlockSpec((1, 2, _DN_W, tm), lambda b, t: (b, 0, 0, t)),
                   tile4(DN_HEADS * DN_CHUNK),
                   pl.BlockSpec((1, 2, _DN_NCH, 8, _DN_W), lambda b, t: (b, 0, t, 0, 0))],
        out_shape=[jax.ShapeDtypeStruct((B, 2, NTOK, _DN_W), f32),
                   jax.ShapeDtypeStruct((B, 2, NTOK, _DN_W), bf16),
                   jax.ShapeDtypeStruct((B, 2, NTOK, _DN_W), bf16),
                   jax.ShapeDtypeStruct((B, 2, _DN_W, NTOK), bf16),
                   jax.ShapeDtypeStruct((B, 2, NTOK, DN_HEADS * DN_CHUNK), bf16),
                   jax.ShapeDtypeStruct((B, 2, NTOK // DN_CHUNK, 8, _DN_W), f32)],
        compiler_params=_vmem_params(nbytes, ("parallel", "parallel")), name="gdn_chunk")(
            uq, uq, uq, uab, uab_t, conv_w, al, dt, al.reshape(ndh, 1), dt.reshape(ndh, 1))


def _gdn_scan_kernel(*refs):
    n_in = 6
    fwd, bwd = refs[:n_in], refs[n_in:2 * n_in]
    of_ref, ob_ref, s_ref = refs[2 * n_in:]
    c = DN_CHUNK

    @pl.when(pl.program_id(1) == 0)
    def _():
        s_ref[...] = jnp.zeros_like(s_ref)

    chains = [(d, h) for d in range(2) for h in range(DN_HEADS)]
    states = [s_ref[d, h] for d, h in chains]
    for step in range(_DN_NCH):
        for n, (d, h) in enumerate(chains):
            u_ref, w_ref, qg_ref, kdt_ref, qk_ref, gl_ref = fwd if d == 0 else bwd
            o_ref = of_ref if d == 0 else ob_ref
            i = step if d == 0 else _DN_NCH - 1 - step
            rows = slice(i * c, (i + 1) * c)
            hs = slice(h * DN_DK, (h + 1) * DN_DK)
            s = states[n]
            sb = s.astype(jnp.bfloat16)
            vn = u_ref[0, 0, rows, hs] - jnp.dot(w_ref[0, 0, rows, hs], sb, preferred_element_type=jnp.float32)
            vb = vn.astype(jnp.bfloat16)
            o_ref[0, rows, hs] = (jnp.dot(qg_ref[0, 0, rows, hs], sb, preferred_element_type=jnp.float32)
                                  + jnp.dot(qk_ref[0, 0, rows, h * c:(h + 1) * c], vb,
                                            preferred_element_type=jnp.float32))
            states[n] = s * gl_ref[0, 0, i, 0:1, hs] + jnp.dot(kdt_ref[0, 0, hs, rows], vb,
                                                             preferred_element_type=jnp.float32)
    for n, (d, h) in enumerate(chains):
        s_ref[d, h] = states[n]


def _gdn_scan(parts, seq):
    u, w, qg, kdt, qk, gl = parts
    B, _, NTOK, _ = u.shape
    tm = TOKEN_TILE
    nt, n_lat = NTOK // tm, seq // tm
    n_ctx = nt - n_lat
    fwd_tile = lambda s: jnp.where(s < n_ctx, n_lat + s, s - n_ctx)
    bwd_tile = lambda s: jnp.where(s < n_ctx, nt - 1 - s, n_lat - 1 - (s - n_ctx))

    def specs(d, tile):
        row = lambda wd: pl.BlockSpec((1, 1, tm, wd), lambda b, s: (b, d, tile(s), 0))
        return [row(_DN_W), row(_DN_W), row(_DN_W),
                pl.BlockSpec((1, 1, _DN_W, tm), lambda b, s: (b, d, 0, tile(s))),
                row(DN_HEADS * DN_CHUNK),
                pl.BlockSpec((1, 1, _DN_NCH, 8, _DN_W), lambda b, s: (b, d, tile(s), 0, 0))]

    out = jax.ShapeDtypeStruct((B, NTOK, _DN_W), jnp.float32)
    return pl.pallas_call(
        _gdn_scan_kernel, grid=(B, nt),
        in_specs=specs(0, fwd_tile) + specs(1, bwd_tile),
        out_specs=[pl.BlockSpec((1, tm, _DN_W), lambda b, s: (b, fwd_tile(s), 0)),
                   pl.BlockSpec((1, tm, _DN_W), lambda b, s: (b, bwd_tile(s), 0))],
        out_shape=[out, out],
        scratch_shapes=[pltpu.VMEM((2, DN_HEADS, DN_DK, DN_DV), jnp.float32)],
        compiler_params=_vmem_params(64 * tm * _DN_W * 4, ("parallel", "arbitrary")), name="gdn_scan")(
            u, w, qg, kdt, qk, gl, u, w, qg, kdt, qk, gl)


def _gdn_out_kernel(of_ref, ob_ref, z_ref, g_ref, y_ref):
    outs = []
    for h in range(DN_HEADS):
        hs = slice(h * DN_DV, (h + 1) * DN_DV)
        o = of_ref[0, :, hs] + ob_ref[0, :, hs]
        on = o * lax.rsqrt(jnp.mean(o * o, axis=-1, keepdims=True) + EPS) * g_ref[...]
        outs.append(on * jax.nn.silu(z_ref[0, :, hs].astype(jnp.float32)))
    y_ref[0] = jnp.concatenate(outs, axis=-1).astype(y_ref.dtype)


def _gdn_out(o_f, o_b, z, g_out):
    B, NTOK, W = o_f.shape
    tm = TOKEN_TILE
    tok = pl.BlockSpec((1, tm, W), lambda b, t: (b, t, 0))
    return pl.pallas_call(
        _gdn_out_kernel, grid=(B, NTOK // tm),
        in_specs=[tok, tok, tok, pl.BlockSpec((1, DN_DV), lambda b, t: (0, 0))],
        out_specs=tok, out_shape=jax.ShapeDtypeStruct((B, NTOK, W), jnp.bfloat16),
        compiler_params=_vmem_params(16 * tm * W * 4, ("parallel", "parallel")), name="gdn_out")(
            o_f, o_b, z, g_out.reshape(1, DN_DV))


def kernel(x, c, ctx, c_ctx, w_mod, b_mod, g_norm1, g_norm2, w_in, a_gq, a_gk, b_conv_w, b_conv_b, b_lam,
           b_wr, b_br, b_wi, b_bi, c_conv_w, c_alog, c_dtb, c_gout, d_rpb, w_branch, w_out, w_router,
           b_router, w_gu, b_gu, w_dn, b_dn, g_final):
    B, S, D = x.shape
    L = ctx.shape[1]
    depth = w_mod.shape[0]
    NTOK = S + L
    assert S % TOKEN_TILE == 0 and L % TOKEN_TILE == 0
    n_lat_tiles = S // TOKEN_TILE
    col = {}
    off = 0
    for name, width in _in_splits(D):
        col[name] = (off, off + width)
        off += width
    bf16 = jnp.bfloat16

    def w_cols(w, names, pad_to=None):
        parts = [w[:, col[n][0]:col[n][1]] for n in names]
        out = parts[0] if len(parts) == 1 else jnp.concatenate(parts, axis=1)
        if pad_to is not None:
            out = jnp.pad(out, ((0, 0), (0, pad_to - out.shape[1])))
        return out.astype(bf16)

    cos_t, sin_t = _rope_tables(S, L)
    sc = jax.nn.silu(c)
    scc = jax.nn.silu(c_ctx)
    xa = jnp.concatenate([x, ctx], axis=1)

    for l in range(depth):
        mod = sc @ w_mod[l] + b_mod[l]
        mod_c = scc @ w_mod[l] + b_mod[l]
        mod_all = jnp.stack([mod.reshape(B, 6, D), jnp.broadcast_to(mod_c.reshape(1, 6, D), (B, 6, D))], axis=1)

        h = _norm_mod(xa, g_norm1[l], mod_all, n_lat_tiles, 0, 1)
        hf = h.reshape(B * NTOK, D)
        proj = lambda names, dtype, tn, pad_to=None: _matmul(
            hf, w_cols(w_in[l], names, pad_to), dtype, tn=tn).reshape(B, NTOK, -1)
        ua = proj(('a_q', 'a_k', 'a_v'), bf16, 768)
        ub = proj(('b_x', 'b_y'), jnp.float32, 512)
        ucq = proj(('c_q', 'c_k', 'c_v'), jnp.float32, 768)
        cz = proj(('c_z',), bf16, 512)
        uab = proj(('c_a', 'c_b'), jnp.float32, V7X_LANES, pad_to=V7X_LANES)
        ud = proj(('d_q', 'd_k', 'd_v'), bf16, 768)
        gate_logits = proj(('gate',), bf16, min(1024, N_BRANCH * D))

        qn, kn = _gqa_prep(ua, cos_t, sin_t, a_gq[l], a_gk[l])
        ya = _gqa_attn(qn, kn, ua, S)
        yd = _na_attn(ud, _na_bias_table(d_rpb[l], S), S)

        yb = _rglru(ub, S, b_conv_w[l], b_conv_b[l], b_lam[l], b_wr[l], b_br[l], b_wi[l], b_bi[l])
        o_f, o_b = _gdn_scan(_gdn_chunk(ucq, uab, jnp.swapaxes(uab, 1, 2), S, c_conv_w[l], c_alog[l], c_dtb[l]), S)
        yc = _gdn_out(o_f, o_b, cz, c_gout[l])
        ys = [ya, yb, yc, yd]

        xa = _merge(ys, gate_logits, w_branch[l].astype(bf16), w_out[l].astype(bf16), xa, mod_all, n_lat_tiles)
        h2, logits = _norm_mod(xa, g_norm2[l], mod_all, n_lat_tiles, 3, 4, router=(w_router[l], b_router[l]))
        xa = _moe(h2.reshape(B * NTOK, D), logits.reshape(B * NTOK, -1),
                  w_gu[l].astype(bf16), b_gu[l], w_dn[l].astype(bf16), b_dn[l], xa, mod_all, S)

    return _final_norm(xa, g_final, S)
```

```python
import functools

import jax
import jax.numpy as jnp
import numpy as np
from jax import lax
from jax.experimental import pallas as pl
from jax.experimental.pallas import tpu as pltpu

GRID_W = 64
HEAD_DIM = 64
EPS = 1e-6

A_HEADS = 8
A_KV_HEADS = 2
A_GROUP = A_HEADS // A_KV_HEADS
A_BLOCK = 128
ROPE_THETA = 10000.0

LRU_WIDTH = 512
LRU_BLOCKS = 8
LRU_BLOCK_W = LRU_WIDTH // LRU_BLOCKS
LRU_C = 8.0

CONV_W = 4
CONV_PAD = ((CONV_W - 1) // 2, CONV_W // 2)

DN_HEADS = 4
DN_DK = 128
DN_DV = 128
DN_CHUNK = 64

NA_HEADS = 8
NA_WIN_R = 8
NA_WIN_C = 16

N_BRANCH = 4
BRANCH_W = 512

TOP_K = 4
SWIGLU_LIMIT = 7.0
SWIGLU_ALPHA = 1.702

V7X_LANES = 128
V7X_MXU_DIM = 256
V7X_VMEM_BYTES = 64 * 1024 * 1024

TOKEN_TILE = 256
MOE_TILE = 256


def _in_splits(d_model):
    return (
        ('a_q', A_HEADS * HEAD_DIM), ('a_k', A_KV_HEADS * HEAD_DIM), ('a_v', A_KV_HEADS * HEAD_DIM),
        ('b_x', LRU_WIDTH), ('b_y', LRU_WIDTH),
        ('c_q', DN_HEADS * DN_DK), ('c_k', DN_HEADS * DN_DK), ('c_v', DN_HEADS * DN_DV),
        ('c_z', DN_HEADS * DN_DV), ('c_a', 2 * DN_HEADS), ('c_b', 2 * DN_HEADS),
        ('d_q', NA_HEADS * HEAD_DIM), ('d_k', NA_HEADS * HEAD_DIM), ('d_v', NA_HEADS * HEAD_DIM),
        ('gate', N_BRANCH * d_model),
    )


def _vmem_params(nbytes, semantics):
    limit = int(min(max(2 * nbytes, 32 * 1024 * 1024), V7X_VMEM_BYTES - 8 * 1024 * 1024))
    return pltpu.CompilerParams(dimension_semantics=semantics, vmem_limit_bytes=limit)


def _norm_mod_kernel(x_ref, g_ref, mod_ref, h_ref, *, shift_idx, scale_idx):
    x = x_ref[0]
    y = x * lax.rsqrt(jnp.mean(x * x, axis=-1, keepdims=True) + EPS) * g_ref[...]
    shift = mod_ref[0, 0, shift_idx:shift_idx + 1, :]
    scale = mod_ref[0, 0, scale_idx:scale_idx + 1, :]
    h_ref[0] = (y * (1 + scale) + shift).astype(h_ref.dtype)


def _norm_mod_router_kernel(x_ref, g_ref, mod_ref, wr_ref, br_ref, h_ref, lg_ref, *, shift_idx, scale_idx):
    x = x_ref[0]
    y = x * lax.rsqrt(jnp.mean(x * x, axis=-1, keepdims=True) + EPS) * g_ref[...]
    shift = mod_ref[0, 0, shift_idx:shift_idx + 1, :]
    scale = mod_ref[0, 0, scale_idx:scale_idx + 1, :]
    h = y * (1 + scale) + shift
    h_ref[0] = h.astype(h_ref.dtype)
    lg_ref[0] = jnp.dot(h, wr_ref[...], preferred_element_type=jnp.float32,
                        precision=lax.Precision.HIGHEST) + br_ref[...]


def _norm_mod(xa, g, mod_all, n_lat_tiles, shift_idx, scale_idx, router=None):
    B, NTOK, D = xa.shape
    tm = TOKEN_TILE
    grid = (B, NTOK // tm)
    x_spec = pl.BlockSpec((1, tm, D), lambda b, t: (b, t, 0))
    g_spec = pl.BlockSpec((1, D), lambda b, t: (0, 0))
    mod_spec = pl.BlockSpec((1, 1, 6, D), lambda b, t: (b, jnp.where(t >= n_lat_tiles, 1, 0), 0, 0))
    h_spec = pl.BlockSpec((1, tm, D), lambda b, t: (b, t, 0))
    params = _vmem_params(8 * tm * D * 4, ("parallel", "parallel"))
    if router is None:
        return pl.pallas_call(
            functools.partial(_norm_mod_kernel, shift_idx=shift_idx, scale_idx=scale_idx),
            grid=grid, in_specs=[x_spec, g_spec, mod_spec], out_specs=h_spec,
            out_shape=jax.ShapeDtypeStruct((B, NTOK, D), jnp.bfloat16),
            compiler_params=params, name="norm_mod")(xa, g.reshape(1, D), mod_all)
    w_router, b_router = router
    E = w_router.shape[1]
    return pl.pallas_call(
        functools.partial(_norm_mod_router_kernel, shift_idx=shift_idx, scale_idx=scale_idx),
        grid=grid,
        in_specs=[x_spec, g_spec, mod_spec, pl.BlockSpec((D, E), lambda b, t: (0, 0)),
                  pl.BlockSpec((1, E), lambda b, t: (0, 0))],
        out_specs=[h_spec, pl.BlockSpec((1, tm, E), lambda b, t: (b, t, 0))],
        out_shape=[jax.ShapeDtypeStruct((B, NTOK, D), jnp.float32),
                   jax.ShapeDtypeStruct((B, NTOK, E), jnp.float32)],
        compiler_params=params, name="norm_mod_router")(xa, g.reshape(1, D), mod_all, w_router,
                                                         b_router.reshape(1, E))


def _mm_kernel(a_ref, w_ref, o_ref):
    o_ref[...] = jnp.dot(a_ref[...], w_ref[...], preferred_element_type=jnp.float32).astype(o_ref.dtype)


def _matmul(a, w, out_dtype, tm=512, tn=512):
    M, K = a.shape
    N = w.shape[1]
    assert M % tm == 0 and N % tn == 0, (M, N, tm, tn)
    nbytes = 2 * (tm * K * a.dtype.itemsize + K * tn * w.dtype.itemsize + tm * tn * 4)
    return pl.pallas_call(
        _mm_kernel, grid=(N // tn, M // tm),
        in_specs=[pl.BlockSpec((tm, K), lambda j, i: (i, 0)), pl.BlockSpec((K, tn), lambda j, i: (0, j))],
        out_specs=pl.BlockSpec((tm, tn), lambda j, i: (i, j)),
        out_shape=jax.ShapeDtypeStruct((M, N), out_dtype),
        compiler_params=_vmem_params(nbytes, ("parallel", "parallel")), name="matmul")(a, w)


def _merge_kernel(ya_ref, yb_ref, yc_ref, yd_ref, gl_ref, wb_ref, wo_ref, x_ref, mod_ref, o_ref, *, d_model):
    m = None
    for i, y_ref in enumerate((ya_ref, yb_ref, yc_ref, yd_ref)):
        gates = jax.nn.sigmoid(gl_ref[0, :, i * d_model:(i + 1) * d_model].astype(jnp.float32))
        p = gates * jnp.dot(y_ref[0], wb_ref[i], preferred_element_type=jnp.float32)
        m = p if m is None else m + p
    out = jnp.dot(m.astype(jnp.bfloat16), wo_ref[...], preferred_element_type=jnp.float32)
    o_ref[0] = x_ref[0] + mod_ref[0, 0, 2:3, :] * out


def _merge(ys, gate_logits, w_branch, w_out, xa, mod_all, n_lat_tiles):
    B, NTOK, D = xa.shape
    tm = TOKEN_TILE
    tok = lambda w: pl.BlockSpec((1, tm, w), lambda b, t: (b, t, 0))
    nbytes = (2 * (4 * tm * BRANCH_W * 2 + tm * 4 * D * 4 + 2 * tm * D * 4)
              + 2 * (N_BRANCH * BRANCH_W * D * 2 + D * D * 2) + 4 * tm * D * 4)
    return pl.pallas_call(
        functools.partial(_merge_kernel, d_model=D), grid=(B, NTOK // tm),
        in_specs=[tok(BRANCH_W)] * 4 + [
            tok(N_BRANCH * D),
            pl.BlockSpec((N_BRANCH, BRANCH_W, D), lambda b, t: (0, 0, 0)),
            pl.BlockSpec((D, D), lambda b, t: (0, 0)),
            tok(D),
            pl.BlockSpec((1, 1, 6, D), lambda b, t: (b, jnp.where(t >= n_lat_tiles, 1, 0), 0, 0))],
        out_specs=tok(D), out_shape=jax.ShapeDtypeStruct((B, NTOK, D), jnp.float32),
        compiler_params=_vmem_params(nbytes, ("parallel", "parallel")), name="merge")(
            *ys, gate_logits, w_branch, w_out, xa, mod_all)


def _moe_kernel(te_ref, nv_ref, x_ref, wgu_ref, bgu_ref, wdn_ref, bdn_ref, o_ref, wgu_bf, wdn_bf, *, d_expert):
    i = pl.program_id(0)

    @pl.when(jnp.logical_or(i == 0, te_ref[i] != te_ref[jnp.maximum(i - 1, 0)]))
    def _():
        wgu_bf[...] = wgu_ref[0].astype(jnp.bfloat16)
        wdn_bf[...] = wdn_ref[0].astype(jnp.bfloat16)

    @pl.when(i < nv_ref[0])
    def _():
        gu = jnp.dot(x_ref[...].astype(jnp.bfloat16), wgu_bf[...], preferred_element_type=jnp.float32) + bgu_ref[0]
        gate = jnp.minimum(gu[:, :d_expert], SWIGLU_LIMIT)
        up = jnp.clip(gu[:, d_expert:], -SWIGLU_LIMIT, SWIGLU_LIMIT)
        act = (up + 1) * gate * jax.nn.sigmoid(SWIGLU_ALPHA * gate)
        y = jnp.dot(act.astype(jnp.bfloat16), wdn_bf[...], preferred_element_type=jnp.float32) + bdn_ref[0]
        o_ref[...] = y.astype(o_ref.dtype)

    @pl.when(i >= nv_ref[0])
    def _():
        o_ref[...] = jnp.zeros_like(o_ref)


def _moe_experts(x_sorted, tile_expert, n_valid, w_gu, b_gu, w_dn, b_dn):
    P, D = x_sorted.shape
    E, _, DE2 = w_gu.shape
    DE = DE2 // 2
    tm = MOE_TILE
    grid_spec = pltpu.PrefetchScalarGridSpec(
        num_scalar_prefetch=2, grid=(P // tm,),
        in_specs=[pl.BlockSpec((tm, D), lambda i, te, nv: (i, 0)),
                  pl.BlockSpec((1, D, DE2), lambda i, te, nv: (te[i], 0, 0)),
                  pl.BlockSpec((1, 1, DE2), lambda i, te, nv: (te[i], 0, 0)),
                  pl.BlockSpec((1, DE, D), lambda i, te, nv: (te[i], 0, 0)),
                  pl.BlockSpec((1, 1, D), lambda i, te, nv: (te[i], 0, 0))],
        out_specs=pl.BlockSpec((tm, D), lambda i, te, nv: (i, 0)),
        scratch_shapes=[pltpu.VMEM((D, DE2), jnp.bfloat16), pltpu.VMEM((DE, D), jnp.bfloat16)])
    nbytes = 2 * (tm * D * 4 + D * DE2 * 4 + DE * D * 4 + tm * D * 4) + (D * DE2 + DE * D) * 2 + tm * DE2 * 4 * 3
    return pl.pallas_call(
        functools.partial(_moe_kernel, d_expert=DE), grid_spec=grid_spec,
        out_shape=jax.ShapeDtypeStruct((P, D), jnp.float32),
        compiler_params=_vmem_params(nbytes, ("arbitrary",)), name="moe_experts")(
            tile_expert, n_valid, x_sorted, w_gu, b_gu.reshape(E, 1, DE2), w_dn, b_dn.reshape(E, 1, D))


_ROUTE_TILE = 256


def _moe_route_kernel(e_ref, base_ref, dest_ref, carry_ref):
    n_exp, n = base_ref.shape[0], e_ref.shape[2]

    @pl.when(pl.program_id(0) == 0)
    def _():
        carry_ref[...] = jnp.zeros_like(carry_ref)

    hot = jnp.where(lax.broadcasted_iota(jnp.int32, (n_exp, n), 0) == e_ref[0], 1.0, 0.0)
    earlier = jnp.where(lax.broadcasted_iota(jnp.int32, (n, n), 0) < lax.broadcasted_iota(jnp.int32, (n, n), 1),
                        1.0, 0.0)
    rank = jnp.dot(hot.astype(jnp.bfloat16), earlier.astype(jnp.bfloat16), preferred_element_type=jnp.float32)
    slot = jnp.sum(hot * (rank + carry_ref[...] + base_ref[...]), axis=0, keepdims=True)
    dest_ref[0] = slot.astype(jnp.int32)
    carry_ref[...] = carry_ref[...] + jnp.sum(hot, axis=1, keepdims=True)


def _moe_route(flat_e, pad_start):
    n_pairs = flat_e.shape[0]
    n_exp = pad_start.shape[0]
    n = _ROUTE_TILE
    assert n_pairs % n == 0 and n_pairs + n_exp * MOE_TILE < 2 ** 24
    dest = pl.pallas_call(
        _moe_route_kernel, grid=(n_pairs // n,),
        in_specs=[pl.BlockSpec((1, 1, n), lambda i: (i, 0, 0)), pl.BlockSpec((n_exp, 1), lambda i: (0, 0))],
        out_specs=pl.BlockSpec((1, 1, n), lambda i: (i, 0, 0)),
        out_shape=jax.ShapeDtypeStruct((n_pairs // n, 1, n), jnp.int32),
        scratch_shapes=[pltpu.VMEM((n_exp, 1), jnp.float32)],
        compiler_params=_vmem_params(16 * n * n * 4, ("arbitrary",)), name="moe_route")(
            flat_e.reshape(n_pairs // n, 1, n), pad_start.astype(jnp.float32).reshape(n_exp, 1))
    return dest.reshape(n_pairs)


_DISPATCH_ROWS = 256
_COMBINE_ROWS = 128


def _row_copy(src_ref, src_row, dst_ref, dst_row, sem):
    return pltpu.make_async_copy(src_ref.at[pl.ds(src_row, 1)], dst_ref.at[pl.ds(dst_row, 1)], sem)


def _moe_dispatch_kernel(dest_ref, h_ref, xs_in_ref, xs_ref, sem):
    del xs_in_ref
    rows = _DISPATCH_ROWS
    base = pl.program_id(0) * rows * TOP_K

    def issue(r, carry):
        for k in range(TOP_K):
            _row_copy(h_ref, r, xs_ref, dest_ref[base + r * TOP_K + k], sem).start()
        return carry

    def drain(r, carry):
        for _ in range(TOP_K):
            _row_copy(h_ref, 0, xs_ref, 0, sem).wait()
        return carry

    lax.fori_loop(0, rows, issue, 0, unroll=8)
    lax.fori_loop(0, rows, drain, 0)


def _moe_dispatch(h2, dest, n_slots):
    T, D = h2.shape
    rows = _DISPATCH_ROWS
    assert T % rows == 0
    grid_spec = pltpu.PrefetchScalarGridSpec(
        num_scalar_prefetch=1, grid=(T // rows,),
        in_specs=[pl.BlockSpec((rows, D), lambda i, d: (i, 0)), pl.BlockSpec(memory_space=pl.ANY)],
        out_specs=pl.BlockSpec(memory_space=pl.ANY),
        scratch_shapes=[pltpu.SemaphoreType.DMA(())])
    return pl.pallas_call(
        _moe_dispatch_kernel, grid_spec=grid_spec, out_shape=jax.ShapeDtypeStruct((n_slots, D), h2.dtype),
        input_output_aliases={2: 0},
        compiler_params=_vmem_params(4 * rows * D * 4, ("arbitrary",)), name="moe_dispatch")(
            dest, h2, jnp.zeros((n_slots, D), h2.dtype))


def _moe_combine_kernel(dest_ref, y_ref, w_ref, x_ref, mod_ref, o_ref, buf, sems):
    rows = _COMBINE_ROWS
    lin = pl.program_id(0) * pl.num_programs(1) + pl.program_id(1)
    n_steps = pl.num_programs(0) * pl.num_programs(1)

    def issue(step, slot):
        def body(r, carry):
            t = step * rows + r
            for k in range(TOP_K):
                pltpu.make_async_copy(y_ref.at[pl.ds(dest_ref[t * TOP_K + k], 1)], buf.at[slot, k, pl.ds(r, 1)],
                                      sems.at[slot]).start()
            return carry
        lax.fori_loop(0, rows, body, 0, unroll=8)

    @pl.when(lin == 0)
    def _():
        issue(0, 0)

    @pl.when(lin + 1 < n_steps)
    def _():
        issue(lin + 1, (lin + 1) % 2)

    slot = lin % 2

    def wait_row(r, carry):
        for k in range(TOP_K):
            pltpu.make_async_copy(y_ref.at[pl.ds(0, 1)], buf.at[slot, k, pl.ds(0, 1)], sems.at[slot]).wait()
        return carry
    lax.fori_loop(0, rows, wait_row, 0)

    f = None
    for k in range(TOP_K):
        term = w_ref[:, k:k + 1] * buf[slot, k]
        f = term if f is None else f + term
    o_ref[0] = x_ref[0] + mod_ref[0, 0, 5:6, :] * f


def _moe_combine(y_sorted, dest, top_w, xa, mod_all, seq):
    B, NTOK, D = xa.shape
    rows = _COMBINE_ROWS
    assert seq % rows == 0 and NTOK % rows == 0
    n_lat = seq // rows
    tpb = NTOK // rows
    grid_spec = pltpu.PrefetchScalarGridSpec(
        num_scalar_prefetch=1, grid=(B, tpb),
        in_specs=[pl.BlockSpec(memory_space=pl.ANY),
                  pl.BlockSpec((rows, TOP_K), lambda b, t, d: (b * tpb + t, 0)),
                  pl.BlockSpec((1, rows, D), lambda b, t, d: (b, t, 0)),
                  pl.BlockSpec((1, 1, 6, D), lambda b, t, d: (b, jnp.where(t >= n_lat, 1, 0), 0, 0))],
        out_specs=pl.BlockSpec((1, rows, D), lambda b, t, d: (b, t, 0)),
        scratch_shapes=[pltpu.VMEM((2, TOP_K, rows, D), jnp.float32), pltpu.SemaphoreType.DMA((2,))])
    nbytes = 2 * TOP_K * rows * D * 4 + 8 * rows * D * 4
    return pl.pallas_call(
        _moe_combine_kernel, grid_spec=grid_spec, out_shape=jax.ShapeDtypeStruct((B, NTOK, D), jnp.float32),
        compiler_params=_vmem_params(nbytes, ("arbitrary", "arbitrary")), name="moe_combine")(
            dest, y_sorted, top_w, xa, mod_all)


def _moe(h2, logits, w_gu, b_gu, w_dn, b_dn, xa, mod_all, seq):
    T, D = h2.shape
    E = logits.shape[1]
    tm = MOE_TILE
    top_v, top_i = lax.top_k(logits, TOP_K)
    top_w = jax.nn.softmax(top_v, axis=-1)
    flat_e = top_i.reshape(-1).astype(jnp.int32)
    n_pairs = T * TOP_K
    n_slots = n_pairs + E * tm
    counts = jnp.sum(jax.nn.one_hot(flat_e, E, dtype=jnp.int32), axis=0)
    padded = ((counts + tm - 1) // tm) * tm
    pad_end = jnp.cumsum(padded)
    n_valid = (pad_end[-1] // tm).astype(jnp.int32).reshape(1)
    tile_start = jnp.arange(n_slots // tm, dtype=jnp.int32) * tm
    tile_expert = jnp.minimum(jnp.searchsorted(pad_end, tile_start, side='right'), E - 1).astype(jnp.int32)
    last_expert = tile_expert[jnp.maximum(n_valid[0] - 1, 0)]
    tile_expert = jnp.where(tile_start < pad_end[-1], tile_expert, last_expert)
    dest = _moe_route(flat_e, pad_end - padded)
    x_sorted = _moe_dispatch(h2, dest, n_slots)
    y_sorted = _moe_experts(x_sorted, tile_expert, n_valid, w_gu, b_gu, w_dn, b_dn)
    return _moe_combine(y_sorted, dest, top_w, xa, mod_all, seq)


def _final_norm_kernel(x_ref, g_ref, o_ref):
    x = x_ref[0]
    o_ref[0] = x * lax.rsqrt(jnp.mean(x * x, axis=-1, keepdims=True) + EPS) * g_ref[...]


def _final_norm(xa, g, seq):
    B, _, D = xa.shape
    tm = TOKEN_TILE
    return pl.pallas_call(
        _final_norm_kernel, grid=(B, seq // tm),
        in_specs=[pl.BlockSpec((1, tm, D), lambda b, t: (b, t, 0)), pl.BlockSpec((1, D), lambda b, t: (0, 0))],
        out_specs=pl.BlockSpec((1, tm, D), lambda b, t: (b, t, 0)),
        out_shape=jax.ShapeDtypeStruct((B, seq, D), jnp.float32),
        compiler_params=_vmem_params(4 * tm * D * 4, ("parallel", "parallel")), name="final_norm")(
            xa, g.reshape(1, D))


def _seg_mean_matrix(width, seg):
    idx = np.arange(width) // seg
    return jnp.asarray((idx[:, None] == idx[None, :]).astype(np.float32) / seg, jnp.bfloat16)


def _seg_mean_sq(x, m_ref):
    sq = x * x
    hi = sq.astype(jnp.bfloat16)
    lo = (sq - hi.astype(jnp.float32)).astype(jnp.bfloat16)
    return (jnp.dot(hi, m_ref[...], preferred_element_type=jnp.float32)
            + jnp.dot(lo, m_ref[...], preferred_element_type=jnp.float32))


def _rope(x, cos, sin_signed):
    width = x.shape[1]
    half = HEAD_DIM // 2
    lane = lax.broadcasted_iota(jnp.int32, x.shape, 1)
    fwd = pltpu.roll(x, width - half, axis=1)
    bwd = pltpu.roll(x, half, axis=1)
    rot = jnp.where((lane & (HEAD_DIM - 1)) < half, fwd, bwd)
    return x * cos + rot * sin_signed


def _rope_tables(seq, n_ctx):
    t = jnp.arange(seq)
    row = (t // GRID_W).astype(jnp.float32)
    col = (t % GRID_W).astype(jnp.float32)
    n_freq = HEAD_DIM // 4
    inv = ROPE_THETA ** (-jnp.arange(n_freq, dtype=jnp.float32) / n_freq)
    ang = jnp.concatenate([row[:, None] * inv[None], col[:, None] * inv[None]], axis=-1)
    cos, sin = jnp.cos(ang), jnp.sin(ang)
    reps = V7X_LANES // HEAD_DIM
    cos_t = jnp.tile(jnp.concatenate([cos, cos], axis=-1), (1, reps))
    sin_t = jnp.tile(jnp.concatenate([-sin, sin], axis=-1), (1, reps))
    cos_t = jnp.concatenate([cos_t, jnp.ones((n_ctx, V7X_LANES), jnp.float32)], axis=0)
    sin_t = jnp.concatenate([sin_t, jnp.zeros((n_ctx, V7X_LANES), jnp.float32)], axis=0)
    return cos_t, sin_t


_A_QW = A_HEADS * HEAD_DIM
_A_KW = A_KV_HEADS * HEAD_DIM


def _gqa_prep_kernel(u_ref, cos_ref, sin_ref, gq_ref, gk_ref, mq_ref, mk_ref, q_ref, k_ref):
    u = u_ref[0].astype(jnp.float32)
    q = u[:, :_A_QW]
    k = u[:, _A_QW:_A_QW + _A_KW]
    qn = q * lax.rsqrt(_seg_mean_sq(q, mq_ref) + EPS) * gq_ref[...]
    kn = k * lax.rsqrt(_seg_mean_sq(k, mk_ref) + EPS) * gk_ref[...]
    cos = cos_ref[...]
    sin = sin_ref[...]
    reps = _A_QW // V7X_LANES
    qr = _rope(qn, jnp.concatenate([cos] * reps, axis=1), jnp.concatenate([sin] * reps, axis=1))
    kr = _rope(kn, cos, sin)
    q_ref[0] = (qr * (HEAD_DIM ** -0.5)).astype(q_ref.dtype)
    k_ref[0] = kr.astype(k_ref.dtype)


def _gqa_prep(ua, cos_t, sin_t, g_q, g_k):
    B, NTOK, W = ua.shape
    tm = TOKEN_TILE
    gq = jnp.tile(g_q, A_HEADS).reshape(1, _A_QW)
    gk = jnp.tile(g_k, A_KV_HEADS).reshape(1, _A_KW)
    mq = _seg_mean_matrix(_A_QW, HEAD_DIM)
    mk = _seg_mean_matrix(_A_KW, HEAD_DIM)
    const = lambda shape: pl.BlockSpec(shape, lambda b, t: (0,) * len(shape))
    return pl.pallas_call(
        _gqa_prep_kernel, grid=(B, NTOK // tm),
        in_specs=[pl.BlockSpec((1, tm, W), lambda b, t: (b, t, 0)),
                  pl.BlockSpec((tm, V7X_LANES), lambda b, t: (t, 0)),
                  pl.BlockSpec((tm, V7X_LANES), lambda b, t: (t, 0)),
                  const((1, _A_QW)), const((1, _A_KW)), const((_A_QW, _A_QW)), const((_A_KW, _A_KW))],
        out_specs=[pl.BlockSpec((1, tm, _A_QW), lambda b, t: (b, t, 0)),
                   pl.BlockSpec((1, tm, _A_KW), lambda b, t: (b, t, 0))],
        out_shape=[jax.ShapeDtypeStruct((B, NTOK, _A_QW), jnp.bfloat16),
                   jax.ShapeDtypeStruct((B, NTOK, _A_KW), jnp.bfloat16)],
        compiler_params=_vmem_params(16 * tm * W * 4, ("parallel", "parallel")), name="gqa_prep")(
            ua, cos_t, sin_t, gq, gk, mq, mk)


def _dot_nt(a, b):
    return lax.dot_general(a, b, (((1,), (1,)), ((), ())), preferred_element_type=jnp.float32)


def _softmax_pv(scores, values):
    m = None
    for s in scores:
        mi = jnp.max(s, axis=-1, keepdims=True)
        m = mi if m is None else jnp.maximum(m, mi)
    acc, den = None, None
    for s, v in zip(scores, values):
        p = jnp.exp(s - m)
        li = jnp.sum(p, axis=-1, keepdims=True)
        oi = jnp.dot(p.astype(jnp.bfloat16), v, preferred_element_type=jnp.float32)
        acc = oi if acc is None else acc + oi
        den = li if den is None else den + li
    return acc / den


def _gqa_attn_kernel(q_ref, k_ref, v_ref, o_ref, *, seq, n_lat_tiles):
    t = pl.program_id(1)
    ntok = k_ref.shape[1]

    def attend(lo):
        outs = []
        for j in range(A_KV_HEADS):
            kv = slice(j * HEAD_DIM, (j + 1) * HEAD_DIM)
            kj = k_ref[0, lo:ntok, kv]
            vj = v_ref[0, lo:ntok, kv]
            for g in range(A_GROUP):
                h = j * A_GROUP + g
                qh = q_ref[0, :, h * HEAD_DIM:(h + 1) * HEAD_DIM]
                outs.append(_softmax_pv([_dot_nt(qh, kj)], [vj]))
        o_ref[0] = jnp.concatenate(outs, axis=-1).astype(o_ref.dtype)

    @pl.when(t < n_lat_tiles)
    def _():
        attend(0)

    @pl.when(t >= n_lat_tiles)
    def _():
        attend(seq)


def _gqa_attn(qn, kn, ua, seq):
    B, NTOK, _ = qn.shape
    tq = TOKEN_TILE
    v_block = (_A_QW + _A_KW) // _A_KW
    nbytes = 4 * NTOK * _A_KW * 2 + 4 * tq * _A_QW * 2 + 6 * tq * NTOK * 4
    return pl.pallas_call(
        functools.partial(_gqa_attn_kernel, seq=seq, n_lat_tiles=seq // tq), grid=(B, NTOK // tq),
        in_specs=[pl.BlockSpec((1, tq, _A_QW), lambda b, t: (b, t, 0)),
                  pl.BlockSpec((1, NTOK, _A_KW), lambda b, t: (b, 0, 0)),
                  pl.BlockSpec((1, NTOK, _A_KW), lambda b, t: (b, 0, v_block))],
        out_specs=pl.BlockSpec((1, tq, _A_QW), lambda b, t: (b, t, 0)),
        out_shape=jax.ShapeDtypeStruct((B, NTOK, _A_QW), jnp.bfloat16),
        compiler_params=_vmem_params(nbytes, ("parallel", "arbitrary")), name="gqa_attn")(qn, kn, ua)


_NA_W = NA_HEADS * HEAD_DIM
_NA_KBLOCK = 3 * TOKEN_TILE
_NA_MASKED = -1e30


def _na_bias_table(rpb, seq):
    rows = seq // GRID_W
    rows_per_tile = TOKEN_TILE // GRID_W
    n_tiles = rows // rows_per_tile
    assert TOKEN_TILE % GRID_W == 0 and rows % rows_per_tile == 0 and n_tiles >= 3 and rows >= NA_WIN_R
    assert rows_per_tile + NA_WIN_R <= _NA_KBLOCK // GRID_W
    key_rows = _NA_KBLOCK // GRID_W
    n_r, n_c = 2 * NA_WIN_R - 1, 2 * NA_WIN_C - 1
    col = np.arange(GRID_W)
    cidx = np.clip(col[None, :] - col[:, None] + NA_WIN_C - 1, 0, n_c - 1)
    c_hot = (np.arange(n_c)[:, None, None] == cidx[None]).astype(np.float32)
    cs = np.clip(col - NA_WIN_C // 2, 0, GRID_W - NA_WIN_C)
    c_ok = (col[None, :] >= cs[:, None]) & (col[None, :] < cs[:, None] + NA_WIN_C)
    tables = []
    for t in (0, 1, n_tiles - 1):
        qr = t * rows_per_tile + np.arange(rows_per_tile)
        kr = int(np.clip(t - 1, 0, n_tiles - 3)) * rows_per_tile + np.arange(key_rows)
        rs = np.clip(qr - NA_WIN_R // 2, 0, rows - NA_WIN_R)
        r_ok = (kr[None, :] >= rs[:, None]) & (kr[None, :] < rs[:, None] + NA_WIN_R)
        ridx = np.clip(kr[None, :] - qr[:, None] + NA_WIN_R - 1, 0, n_r - 1)
        r_hot = (np.arange(n_r)[:, None, None] == ridx[None]).astype(np.float32)
        b = jnp.einsum('rak,hrc,cqj->haqkj', r_hot, rpb, c_hot, precision=lax.Precision.HIGHEST)
        valid = r_ok[:, None, :, None] & c_ok[None, :, None, :]
        tables.append(jnp.where(valid[None], b, _NA_MASKED).reshape(NA_HEADS, TOKEN_TILE, _NA_KBLOCK))
    return jnp.stack(tables)


def _na_attn_kernel(q_ref, k_ref, v_ref, bias_ref, o_ref, *, seq, n_lat_tiles):
    t = pl.program_id(1)
    ntok = k_ref.shape[1]
    scale = HEAD_DIM ** -0.5

    @pl.when(t < n_lat_tiles)
    def _():
        k_off = pl.multiple_of(jnp.clip(t - 1, 0, n_lat_tiles - 3) * TOKEN_TILE, TOKEN_TILE)
        outs = []
        for h in range(NA_HEADS):
            hs = slice(h * HEAD_DIM, (h + 1) * HEAD_DIM)
            qh = q_ref[0, :, hs] * scale
            s_nb = _dot_nt(qh, k_ref[0, pl.ds(k_off, _NA_KBLOCK), hs]) + bias_ref[0, h]
            s_ctx = _dot_nt(qh, k_ref[0, seq:ntok, hs])
            outs.append(_softmax_pv([s_nb, s_ctx], [v_ref[0, pl.ds(k_off, _NA_KBLOCK), hs], v_ref[0, seq:ntok, hs]]))
        o_ref[0] = jnp.concatenate(outs, axis=-1).astype(o_ref.dtype)

    @pl.when(t >= n_lat_tiles)
    def _():
        outs = []
        for h in range(NA_HEADS):
            hs = slice(h * HEAD_DIM, (h + 1) * HEAD_DIM)
            qh = q_ref[0, :, hs] * scale
            outs.append(_softmax_pv([_dot_nt(qh, k_ref[0, seq:ntok, hs])], [v_ref[0, seq:ntok, hs]]))
        o_ref[0] = jnp.concatenate(outs, axis=-1).astype(o_ref.dtype)


def _na_attn(ud, bias, seq):
    B, NTOK, _ = ud.shape
    tq = TOKEN_TILE
    n_lat_tiles = seq // tq
    nbytes = (4 * NTOK * _NA_W * 2 + 4 * tq * _NA_W * 2 + 2 * NA_HEADS * tq * _NA_KBLOCK * 4
              + 8 * tq * (_NA_KBLOCK + NTOK - seq) * 4)
    return pl.pallas_call(
        functools.partial(_na_attn_kernel, seq=seq, n_lat_tiles=n_lat_tiles), grid=(B, NTOK // tq),
        in_specs=[pl.BlockSpec((1, tq, _NA_W), lambda b, t: (b, t, 0)),
                  pl.BlockSpec((1, NTOK, _NA_W), lambda b, t: (b, 0, 1)),
                  pl.BlockSpec((1, NTOK, _NA_W), lambda b, t: (b, 0, 2)),
                  pl.BlockSpec((1, NA_HEADS, tq, _NA_KBLOCK),
                               lambda b, t: (jnp.where(t == 0, 0, jnp.where(t >= n_lat_tiles - 1, 2, 1)), 0, 0, 0))],
        out_specs=pl.BlockSpec((1, tq, _NA_W), lambda b, t: (b, t, 0)),
        out_shape=jax.ShapeDtypeStruct((B, NTOK, _NA_W), jnp.bfloat16),
        compiler_params=_vmem_params(nbytes, ("parallel", "arbitrary")), name="na_attn")(ud, ud, ud, bias)


_HALO = 8


def _dwconv_tile(prev8, cur, next8, w, row0, seq, ntok):
    x = jnp.concatenate([prev8, cur, next8], axis=0)
    n = x.shape[0]
    t = row0 - _HALO + lax.broadcasted_iota(jnp.int32, (n, 1), 0)
    keep = lambda hit: jnp.where(hit, 0.0, 1.0)
    first = keep(t == 0) * keep(t == seq)
    last1 = keep(t == seq - 1) * keep(t == ntok - 1)
    last2 = last1 * keep(t == seq - 2) * keep(t == ntok - 2)
    xm1 = pltpu.roll(x, 1, axis=0) * first
    xp1 = pltpu.roll(x, n - 1, axis=0) * last1
    xp2 = pltpu.roll(x, n - 2, axis=0) * last2
    y = w[0:1] * xm1 + w[1:2] * x + w[2:3] * xp1 + w[3:4] * xp2
    return y[_HALO:n - _HALO]


def _softplus(z):
    return jnp.log1p(jnp.exp(-jnp.abs(z))) + jnp.maximum(z, 0.0)


_LRU_CB = V7X_LANES


def _scan_tile(a, b, h_prev, reverse):
    n = a.shape[0]
    row = lax.broadcasted_iota(jnp.int32, (n, 1), 0)
    d = 1
    while d < n:
        if reverse:
            inside = row < n - d
            a_sh = pltpu.roll(a, n - d, axis=0)
            b_sh = pltpu.roll(b, n - d, axis=0)
        else:
            inside = row >= d
            a_sh = pltpu.roll(a, d, axis=0)
            b_sh = pltpu.roll(b, d, axis=0)
        b = jnp.where(inside, a * b_sh, 0.0) + b
        a = jnp.where(inside, a * a_sh, a)
        d *= 2
    h = a * h_prev + b
    return h, (h[0:1] if reverse else h[n - 1:n])


def _rglru_kernel(x_ref, y_ref, cw_ref, cb_ref, lam_ref, wr_ref, br_ref, wi_ref, bi_ref, o_ref, hf_ref,
                  *, seq, ntok):
    tm = TOKEN_TILE
    n_tiles, n_lat = ntok // tm, seq // tm
    n_ctx = n_tiles - n_lat

    def coeffs(i, d):
        r0 = pl.multiple_of(i * tm, tm)
        p0 = pl.multiple_of(jnp.maximum(r0 - _HALO, 0), _HALO)
        n0 = pl.multiple_of(jnp.minimum(r0 + tm, ntok - _HALO), _HALO)
        xc = _dwconv_tile(x_ref[0, pl.ds(p0, _HALO), :], x_ref[0, pl.ds(r0, tm), :], x_ref[0, pl.ds(n0, _HALO), :],
                          cw_ref[...], r0, seq, ntok) + cb_ref[...]
        xb = xc.astype(jnp.bfloat16)
        r = jax.nn.sigmoid(jnp.dot(xb, wr_ref[d, 0], preferred_element_type=jnp.float32) + br_ref[d:d + 1])
        g = jax.nn.sigmoid(jnp.dot(xb, wi_ref[d, 0], preferred_element_type=jnp.float32) + bi_ref[d:d + 1])
        a = jnp.exp(-LRU_C * r * _softplus(-lam_ref[d:d + 1]))
        return r0, a, jnp.sqrt(1.0 - a * a) * (g * xc)

    def fwd(k, h):
        i = jnp.where(k < n_ctx, n_lat + k, k - n_ctx)
        r0, a, b = coeffs(i, 0)
        ht, h = _scan_tile(a, b, h, reverse=False)
        hf_ref[pl.ds(r0, tm), :] = ht
        return h

    def bwd(k, h):
        i = jnp.where(k < n_ctx, n_tiles - 1 - k, n_lat - 1 - (k - n_ctx))
        r0, a, b = coeffs(i, 1)
        ht, h = _scan_tile(a, b, h, reverse=True)
        yv = y_ref[0, pl.ds(r0, tm), :]
        o_ref[0, pl.ds(r0, tm), :] = ((hf_ref[pl.ds(r0, tm), :] + ht) * jax.nn.gelu(yv)).astype(o_ref.dtype)
        return h

    h0 = jnp.zeros((1, _LRU_CB), jnp.float32)
    lax.fori_loop(0, n_tiles, fwd, h0)
    lax.fori_loop(0, n_tiles, bwd, h0)


def _rglru(ub, seq, conv_w, conv_b, lam, w_r, b_r, w_i, b_i):
    B, NTOK, _ = ub.shape
    cb = _LRU_CB
    ncb = LRU_WIDTH // cb
    per = cb // LRU_BLOCK_W

    def block_diag(w):
        w = w.reshape(2, ncb, per, LRU_BLOCK_W, LRU_BLOCK_W)
        eye = jnp.eye(per, dtype=w.dtype)
        return jnp.einsum('dnpij,pq->dnpiqj', w, eye).reshape(2, ncb, cb, cb).astype(jnp.bfloat16)

    vec = lambda rows: pl.BlockSpec((rows, cb), lambda b, c: (0, c))
    wspec = pl.BlockSpec((2, 1, cb, cb), lambda b, c: (0, c, 0, 0))
    nbytes = 2 * (2 * NTOK * cb * 4 + NTOK * cb * 2) + NTOK * cb * 4 + 64 * TOKEN_TILE * cb * 4
    return pl.pallas_call(
        functools.partial(_rglru_kernel, seq=seq, ntok=NTOK), grid=(B, ncb),
        in_specs=[pl.BlockSpec((1, NTOK, cb), lambda b, c: (b, 0, c)),
                  pl.BlockSpec((1, NTOK, cb), lambda b, c: (b, 0, ncb + c)),
                  vec(CONV_W), vec(1), vec(2), wspec, vec(2), wspec, vec(2)],
        out_specs=pl.BlockSpec((1, NTOK, cb), lambda b, c: (b, 0, c)),
        out_shape=jax.ShapeDtypeStruct((B, NTOK, LRU_WIDTH), jnp.bfloat16),
        scratch_shapes=[pltpu.VMEM((NTOK, cb), jnp.float32)],
        compiler_params=_vmem_params(nbytes, ("parallel", "parallel")), name="rglru")(
            ub, ub, conv_w, conv_b.reshape(1, -1), lam, block_diag(w_r), b_r, block_diag(w_i), b_i)


_DN_W = DN_HEADS * DN_DK
_DN_NCH = TOKEN_TILE // DN_CHUNK
_DN_NDH = 2 * DN_HEADS


def _l2n(x):
    return x * lax.rsqrt(jnp.sum(x * x, axis=-1, keepdims=True) + EPS)


def _dot_f32(a, b):
    return jnp.dot(a, b, preferred_element_type=jnp.float32, precision=lax.Precision.HIGHEST)


def _bdot(a, b):
    return jnp.dot(a.astype(jnp.bfloat16), b.astype(jnp.bfloat16), preferred_element_type=jnp.float32)


def _diag_blocks(m):
    c = DN_CHUNK
    return jnp.concatenate([m[i * c:(i + 1) * c, i * c:(i + 1) * c] for i in range(_DN_NCH)], axis=0)


def _gdn_chunk_kernel(prev_ref, cur_ref, next_ref, ab_ref, abt_ref, cw_ref, al_ref, dt_ref, alt_ref, dtt_ref,
                      u_ref, w_ref, qg_ref, kdt_ref, qk_ref, gl_ref, *, seq, ntok):
    tm = TOKEN_TILE
    c = DN_CHUNK
    row0 = pl.program_id(1) * tm
    qkv = jax.nn.silu(_dwconv_tile(prev_ref[0], cur_ref[0], next_ref[0], cw_ref[...], row0, seq, ntok))

    ndh = _DN_NDH
    g_col = -jnp.exp(al_ref[...]) * _softplus(ab_ref[0, :, 0:ndh] + dt_ref[...])
    beta_col = jax.nn.sigmoid(ab_ref[0, :, ndh:2 * ndh])
    g_row = -jnp.exp(alt_ref[...]) * _softplus(abt_ref[0, 0:ndh, :] + dtt_ref[...])

    ri = lax.broadcasted_iota(jnp.int32, (tm, tm), 0)
    ci = lax.broadcasted_iota(jnp.int32, (tm, tm), 1)
    blk = lambda v, size: jnp.right_shift(v, size.bit_length() - 1)
    same = blk(ri, c) == blk(ci, c)
    one = lambda m: jnp.where(m, 1.0, 0.0)
    low = one(same) * one(ri >= ci)
    upp = one(same) * one(ri <= ci)
    low_s = one(same) * one(ri > ci)
    upp_s = one(same) * one(ri < ci)
    samef = one(same)
    eye = one(ri == ci)
    base = 8
    blk8 = one(blk(ri, base) == blk(ci, base))
    merge = ([], [])
    n = base
    while n < c:
        pair = one(blk(ri, 2 * n) == blk(ci, 2 * n))
        r_odd, c_odd = one((blk(ri, n) & 1) == 1), one((blk(ci, n) & 1) == 1)
        merge[0].append(pair * r_odd * (1.0 - c_odd))
        merge[1].append(pair * (1.0 - r_odd) * c_odd)
        n *= 2

    nh = DN_HEADS
    is_fwd_col = lax.broadcasted_iota(jnp.int32, (tm, ndh), 1) < nh
    is_fwd_row = lax.broadcasted_iota(jnp.int32, (ndh, tm), 0) < nh
    gc_col = jnp.where(is_fwd_col, _dot_f32(low, g_col), _dot_f32(upp, g_col))
    gc_row = jnp.where(is_fwd_row, _dot_f32(g_row, upp), _dot_f32(g_row, low))
    gsum_col = _dot_f32(samef, g_col)

    for h in range(nh):
        hs = slice(h * DN_DK, (h + 1) * DN_DK)
        qh = _l2n(qkv[:, h * DN_DK:(h + 1) * DN_DK]) * (DN_DK ** -0.5)
        kh = _l2n(qkv[:, _DN_W + h * DN_DK:_DN_W + (h + 1) * DN_DK])
        vh = qkv[:, 2 * _DN_W + h * DN_DV:2 * _DN_W + (h + 1) * DN_DV]
        kk = _dot_nt(kh.astype(jnp.bfloat16), kh.astype(jnp.bfloat16))
        qk = _dot_nt(qh.astype(jnp.bfloat16), kh.astype(jnp.bfloat16))
        for d in range(2):
            j = d * nh + h
            mask, mask_s = (low, low_s) if d == 0 else (upp, upp_s)
            gcc = gc_col[:, j:j + 1]
            diff = gcc - gc_row[j:j + 1, :]
            dec = mask * jnp.exp(mask * diff)
            beta = beta_col[:, j:j + 1]
            egc = jnp.exp(gcc)
            kb = kh * beta
            lm = mask_s * (beta * kk) * dec
            p = lm * blk8
            p2 = _bdot(p, p)
            tinv = eye - p
            tinv = tinv + _bdot(tinv, p2)
            tinv = tinv + _bdot(tinv, _bdot(p2, p2))
            for mm in merge[d]:
                tinv = tinv - _bdot(_bdot(tinv, lm * mm), tinv)
            x = _bdot(tinv, jnp.concatenate([vh * beta, kb * egc], axis=1))
            u_ref[0, d, :, hs] = x[:, :DN_DV]
            w_ref[0, d, :, hs] = x[:, DN_DV:].astype(w_ref.dtype)
            qg_ref[0, d, :, hs] = (qh * egc).astype(qg_ref.dtype)
            kd = kh * jnp.exp(gsum_col[:, j:j + 1] - gcc)
            kdt_ref[0, d, hs, :] = kd.T.astype(kdt_ref.dtype)
            qk_ref[0, d, :, h * c:(h + 1) * c] = _diag_blocks(qk * dec).astype(qk_ref.dtype)
            gl = jnp.exp(gsum_col[:, j:j + 1])
            for i in range(_DN_NCH):
                gl_ref[0, d, i, :, hs] = jnp.broadcast_to(gl[i * c:i * c + 8], (8, DN_DK))


def _gdn_chunk(uq, uab, uab_t, seq, conv_w, a_log, dt_bias):
    B, NTOK, W = uq.shape
    tm = TOKEN_TILE
    nt = NTOK // tm
    hb = tm // _HALO
    ndh = _DN_NDH
    al, dt = a_log.reshape(1, ndh), dt_bias.reshape(1, ndh)
    const = lambda shape: pl.BlockSpec(shape, lambda b, t: (0,) * len(shape))
    tile4 = lambda w: pl.BlockSpec((1, 2, tm, w), lambda b, t: (b, 0, t, 0))
    f32, bf16 = jnp.float32, jnp.bfloat16
    nbytes = 2 * (tm + 16) * W * 4 * 6 + 2 * 2 * tm * (_DN_W * 12 + 256 * 2) + 40 * tm * tm * 4
    return pl.pallas_call(
        functools.partial(_gdn_chunk_kernel, seq=seq, ntok=NTOK), grid=(B, nt),
        in_specs=[pl.BlockSpec((1, _HALO, W), lambda b, t: (b, jnp.maximum(t * hb - 1, 0), 0)),
                  pl.BlockSpec((1, tm, W), lambda b, t: (b, t, 0)),
                  pl.BlockSpec((1, _HALO, W), lambda b, t: (b, jnp.minimum((t + 1) * hb, nt * hb - 1), 0)),
                  pl.BlockSpec((1, tm, V7X_LANES), lambda b, t: (b, t, 0)),
                  pl.BlockSpec((1, V7X_LANES, tm), lambda b, t: (b, 0, t)),
                  const((CONV_W, W)), const((1, ndh)), const((1, ndh)), const((ndh, 1)), const((ndh, 1))],
        out_specs=[tile4(_DN_W), tile4(_DN_W), tile4(_DN_W),
                   pl.BlockSpec((1, 2, _DN_W, tm), lambda b, t: (b, 0, 0, t)),
                   tile4(DN_HEADS * DN_CHUNK),
                   pl.BlockSpec((1, 2, _DN_NCH, 8, _DN_W), lambda b, t: (b, 0, t, 0, 0))],
        out_shape=[jax.ShapeDtypeStruct((B, 2, NTOK, _DN_W), f32),
                   jax.ShapeDtypeStruct((B, 2, NTOK, _DN_W), bf16),
                   jax.ShapeDtypeStruct((B, 2, NTOK, _DN_W), bf16),
                   jax.ShapeDtypeStruct((B, 2, _DN_W, NTOK), bf16),
                   jax.ShapeDtypeStruct((B, 2, NTOK, DN_HEADS * DN_CHUNK), bf16),
                   jax.ShapeDtypeStruct((B, 2, NTOK // DN_CHUNK, 8, _DN_W), f32)],
        compiler_params=_vmem_params(nbytes, ("parallel", "parallel")), name="gdn_chunk")(
            uq, uq, uq, uab, uab_t, conv_w, al, dt, al.reshape(ndh, 1), dt.reshape(ndh, 1))


def _gdn_scan_kernel(*refs):
    n_in = 6
    fwd, bwd = refs[:n_in], refs[n_in:2 * n_in]
    of_ref, ob_ref, s_ref = refs[2 * n_in:]
    c = DN_CHUNK

    @pl.when(pl.program_id(1) == 0)
    def _():
        s_ref[...] = jnp.zeros_like(s_ref)

    chains = [(d, h) for d in range(2) for h in range(DN_HEADS)]
    states = [s_ref[d, h] for d, h in chains]
    for step in range(_DN_NCH):
        for n, (d, h) in enumerate(chains):
            u_ref, w_ref, qg_ref, kdt_ref, qk_ref, gl_ref = fwd if d == 0 else bwd
            o_ref = of_ref if d == 0 else ob_ref
            i = step if d == 0 else _DN_NCH - 1 - step
            rows = slice(i * c, (i + 1) * c)
            hs = slice(h * DN_DK, (h + 1) * DN_DK)
            s = states[n]
            sb = s.astype(jnp.bfloat16)
            vn = u_ref[0, 0, rows, hs] - jnp.dot(w_ref[0, 0, rows, hs], sb, preferred_element_type=jnp.float32)
            vb = vn.astype(jnp.bfloat16)
            o_ref[0, rows, hs] = (jnp.dot(qg_ref[0, 0, rows, hs], sb, preferred_element_type=jnp.float32)
                                  + jnp.dot(qk_ref[0, 0, rows, h * c:(h + 1) * c], vb,
                                            preferred_element_type=jnp.float32))
            states[n] = s * gl_ref[0, 0, i, 0:1, hs] + jnp.dot(kdt_ref[0, 0, hs, rows], vb,
                                                             preferred_element_type=jnp.float32)
    for n, (d, h) in enumerate(chains):
        s_ref[d, h] = states[n]


def _gdn_scan(parts, seq):
    u, w, qg, kdt, qk, gl = parts
    B, _, NTOK, _ = u.shape
    tm = TOKEN_TILE
    nt, n_lat = NTOK // tm, seq // tm
    n_ctx = nt - n_lat
    fwd_tile = lambda s: jnp.where(s < n_ctx, n_lat + s, s - n_ctx)
    bwd_tile = lambda s: jnp.where(s < n_ctx, nt - 1 - s, n_lat - 1 - (s - n_ctx))

    def specs(d, tile):
        row = lambda wd: pl.BlockSpec((1, 1, tm, wd), lambda b, s: (b, d, tile(s), 0))
        return [row(_DN_W), row(_DN_W), row(_DN_W),
                pl.BlockSpec((1, 1, _DN_W, tm), lambda b, s: (b, d, 0, tile(s))),
                row(DN_HEADS * DN_CHUNK),
                pl.BlockSpec((1, 1, _DN_NCH, 8, _DN_W), lambda b, s: (b, d, tile(s), 0, 0))]

    out = jax.ShapeDtypeStruct((B, NTOK, _DN_W), jnp.float32)
    return pl.pallas_call(
        _gdn_scan_kernel, grid=(B, nt),
        in_specs=specs(0, fwd_tile) + specs(1, bwd_tile),
        out_specs=[pl.BlockSpec((1, tm, _DN_W), lambda b, s: (b, fwd_tile(s), 0)),
                   pl.BlockSpec((1, tm, _DN_W), lambda b, s: (b, bwd_tile(s), 0))],
        out_shape=[out, out],
        scratch_shapes=[pltpu.VMEM((2, DN_HEADS, DN_DK, DN_DV), jnp.float32)],
        compiler_params=_vmem_params(64 * tm * _DN_W * 4, ("parallel", "arbitrary")), name="gdn_scan")(
            u, w, qg, kdt, qk, gl, u, w, qg, kdt, qk, gl)


def _gdn_out_kernel(of_ref, ob_ref, z_ref, g_ref, y_ref):
    outs = []
    for h in range(DN_HEADS):
        hs = slice(h * DN_DV, (h + 1) * DN_DV)
        o = of_ref[0, :, hs] + ob_ref[0, :, hs]
        on = o * lax.rsqrt(jnp.mean(o * o, axis=-1, keepdims=True) + EPS) * g_ref[...]
        outs.append(on * jax.nn.silu(z_ref[0, :, hs].astype(jnp.float32)))
    y_ref[0] = jnp.concatenate(outs, axis=-1).astype(y_ref.dtype)


def _gdn_out(o_f, o_b, z, g_out):
    B, NTOK, W = o_f.shape
    tm = TOKEN_TILE
    tok = pl.BlockSpec((1, tm, W), lambda b, t: (b, t, 0))
    return pl.pallas_call(
        _gdn_out_kernel, grid=(B, NTOK // tm),
        in_specs=[tok, tok, tok, pl.BlockSpec((1, DN_DV), lambda b, t: (0, 0))],
        out_specs=tok, out_shape=jax.ShapeDtypeStruct((B, NTOK, W), jnp.bfloat16),
        compiler_params=_vmem_params(16 * tm * W * 4, ("parallel", "parallel")), name="gdn_out")(
            o_f, o_b, z, g_out.reshape(1, DN_DV))


def kernel(x, c, ctx, c_ctx, w_mod, b_mod, g_norm1, g_norm2, w_in, a_gq, a_gk, b_conv_w, b_conv_b, b_lam,
           b_wr, b_br, b_wi, b_bi, c_conv_w, c_alog, c_dtb, c_gout, d_rpb, w_branch, w_out, w_router,
           b_router, w_gu, b_gu, w_dn, b_dn, g_final):
    B, S, D = x.shape
    L = ctx.shape[1]
    depth = w_mod.shape[0]
    NTOK = S + L
    assert S % TOKEN_TILE == 0 and L % TOKEN_TILE == 0
    n_lat_tiles = S // TOKEN_TILE
    col = {}
    off = 0
    for name, width in _in_splits(D):
        col[name] = (off, off + width)
        off += width
    bf16 = jnp.bfloat16

    def w_cols(w, names, pad_to=None):
        parts = [w[:, col[n][0]:col[n][1]] for n in names]
        out = parts[0] if len(parts) == 1 else jnp.concatenate(parts, axis=1)
        if pad_to is not None:
            out = jnp.pad(out, ((0, 0), (0, pad_to - out.shape[1])))
        return out.astype(bf16)

    cos_t, sin_t = _rope_tables(S, L)
    sc = jax.nn.silu(c)
    scc = jax.nn.silu(c_ctx)
    xa = jnp.concatenate([x, ctx], axis=1)

    for l in range(depth):
        mod = sc @ w_mod[l] + b_mod[l]
        mod_c = scc @ w_mod[l] + b_mod[l]
        mod_all = jnp.stack([mod.reshape(B, 6, D), jnp.broadcast_to(mod_c.reshape(1, 6, D), (B, 6, D))], axis=1)

        h = _norm_mod(xa, g_norm1[l], mod_all, n_lat_tiles, 0, 1)
        hf = h.reshape(B * NTOK, D)
        proj = lambda names, dtype, tn, pad_to=None: _matmul(
            hf, w_cols(w_in[l], names, pad_to), dtype, tn=tn).reshape(B, NTOK, -1)
        ua = proj(('a_q', 'a_k', 'a_v'), bf16, 768)
        ub = proj(('b_x', 'b_y'), jnp.float32, 512)
        ucq = proj(('c_q', 'c_k', 'c_v'), jnp.float32, 768)
        cz = proj(('c_z',), bf16, 512)
        uab = proj(('c_a', 'c_b'), jnp.float32, V7X_LANES, pad_to=V7X_LANES)
        ud = proj(('d_q', 'd_k', 'd_v'), bf16, 768)
        gate_logits = proj(('gate',), bf16, min(1024, N_BRANCH * D))

        qn, kn = _gqa_prep(ua, cos_t, sin_t, a_gq[l], a_gk[l])
        ya = _gqa_attn(qn, kn, ua, S)
        yd = _na_attn(ud, _na_bias_table(d_rpb[l], S), S)

        yb = _rglru(ub, S, b_conv_w[l], b_conv_b[l], b_lam[l], b_wr[l], b_br[l], b_wi[l], b_bi[l])
        o_f, o_b = _gdn_scan(_gdn_chunk(ucq, uab, jnp.swapaxes(uab, 1, 2), S, c_conv_w[l], c_alog[l], c_dtb[l]), S)
        yc = _gdn_out(o_f, o_b, cz, c_gout[l])
        ys = [ya, yb, yc, yd]

        xa = _merge(ys, gate_logits, w_branch[l].astype(bf16), w_out[l].astype(bf16), xa, mod_all, n_lat_tiles)
        h2, logits = _norm_mod(xa, g_norm2[l], mod_all, n_lat_tiles, 3, 4, router=(w_router[l], b_router[l]))
        xa = _moe(h2.reshape(B * NTOK, D), logits.reshape(B * NTOK, -1),
                  w_gu[l], b_gu[l], w_dn[l], b_dn[l], xa, mod_all, S)

    return _final_norm(xa, g_final, S)
```
